```python
import math
import jax, jax.numpy as jnp
from jax import lax
import numpy as np

D_MODEL = 1024
BATCH = 8
SEQ = 4096
DEPTH = 4

HEAD_DIM = 64
A_WIDTH = D_MODEL // 2
B_WIDTH = D_MODEL // 2
A_CONV = 31
B_CONV = 3
EVEN_SPLIT = (A_WIDTH, A_WIDTH, B_WIDTH, B_WIDTH, B_WIDTH)
EVEN_IN = sum(EVEN_SPLIT)
EVEN_MIX = A_WIDTH + B_WIDTH
C_HEADS = 8
D_HEADS = 4
D_VDIM = 2 * HEAD_DIM
DILATED_PAIRS = ((128, 1), (512, 4), (2048, 16))
ATTN_BLOCK = 128
RET_CHUNK = 128
ODD_SPLIT = (C_HEADS * HEAD_DIM,) * 3 + (D_HEADS * HEAD_DIM,) * 2 + (D_HEADS * D_VDIM,) * 2
ODD_IN = sum(ODD_SPLIT)
ODD_MIX = C_HEADS * HEAD_DIM + D_HEADS * D_VDIM
D_FF = 2816
FFN_CONV = 3
ROPE_THETA = 10000.0
NORM_EPS = 1e-6
NEG_BIG = -1e30
N_EVEN = (DEPTH + 1) // 2
N_ODD = DEPTH // 2

kernel_name = 'hybrid_conformer_shortconv_dilated_retention_trunk'


def rmsnorm(x, g):
    xf = x.astype(jnp.float32)
    y = xf * lax.rsqrt(jnp.mean(xf * xf, axis=-1, keepdims=True) + NORM_EPS)
    return (y * g.astype(jnp.float32)).astype(x.dtype)


def layernorm(x, g, b):
    xf = x.astype(jnp.float32)
    mu = jnp.mean(xf, axis=-1, keepdims=True)
    xc = xf - mu
    y = xc * lax.rsqrt(jnp.mean(xc * xc, axis=-1, keepdims=True) + NORM_EPS)
    return (y * g.astype(jnp.float32) + b.astype(jnp.float32)).astype(x.dtype)


def head_norm(y):
    mu = jnp.mean(y, axis=-1, keepdims=True)
    yc = y - mu
    return yc * lax.rsqrt(jnp.mean(yc * yc, axis=-1, keepdims=True) + NORM_EPS)


def split_cols(p, sizes):
    return jnp.split(p, np.cumsum(np.array(sizes))[:-1].tolist(), axis=-1)


def causal_dwconv(x, w):
    K, C = w.shape
    return lax.conv_general_dilated(
        x, w[:, None, :].astype(x.dtype), window_strides=(1,), padding=[(K - 1, 0)],
        dimension_numbers=('NWC', 'WIO', 'NWC'), feature_group_count=C)


def rope(x):
    S, hd = x.shape[1], x.shape[-1]
    inv = ROPE_THETA ** (-jnp.arange(0, hd, 2, dtype=jnp.float32) / hd)
    ang = jnp.arange(S, dtype=jnp.float32)[:, None] * inv[None, :]
    cos = jnp.cos(ang)[None, :, None, :]
    sin = jnp.sin(ang)[None, :, None, :]
    xf = x.astype(jnp.float32)
    x1, x2 = xf[..., : hd // 2], xf[..., hd // 2:]
    return jnp.concatenate([x1 * cos - x2 * sin, x1 * sin + x2 * cos], axis=-1).astype(x.dtype)


def dilated_branch(q, k, v, window, dilation):
    Bsz, S, H, hd = q.shape
    L = S // dilation
    W = window // dilation
    nb = -(-L // ATTN_BLOCK)
    Lp = nb * ATTN_BLOCK

    def sub(t):
        t = t.reshape(Bsz, L, dilation, H, hd)
        return jnp.pad(t, ((0, 0), (0, Lp - L), (0, 0), (0, 0), (0, 0)))

    def band(t):
        t = jnp.pad(t, ((0, 0), (ATTN_BLOCK, 0), (0, 0), (0, 0), (0, 0)))
        t = t.reshape(Bsz, nb + 1, ATTN_BLOCK, dilation, H, hd)
        return jnp.concatenate([t[:, :-1], t[:, 1:]], axis=2)

    qb = sub(q).reshape(Bsz, nb, ATTN_BLOCK, dilation, H, hd)
    kb = band(sub(k))
    vb = band(sub(v))
    s = jnp.einsum('bnqrhd,bnkrhd->bnrhqk', qb, kb)
    qi = jnp.arange(ATTN_BLOCK)[:, None]
    ki = jnp.arange(2 * ATTN_BLOCK)[None, :]
    rel = qi + ATTN_BLOCK - ki
    kpos = jnp.arange(nb)[:, None, None] * ATTN_BLOCK + ki - ATTN_BLOCK
    valid = (rel >= 0) & (rel <= W) & (kpos >= 0)
    s = jnp.where(valid[None, :, None, None], s, NEG_BIG)
    m = jnp.max(s, axis=-1, keepdims=True)
    p = jnp.exp(s - m)
    den = jnp.sum(p, axis=-1)
    num = jnp.einsum('bnrhqk,bnkrhd->bnqrhd', p, vb)
    num = num.reshape(Bsz, Lp, dilation, H, hd)[:, :L].reshape(Bsz, S, H, hd)

    def rows(t):
        t = jnp.transpose(t, (0, 1, 4, 2, 3)).reshape(Bsz, Lp, dilation, H)
        return t[:, :L].reshape(Bsz, S, H)

    return num, rows(den), rows(m[..., 0])


def dilated_attention(q, k, v):
    q = q.astype(jnp.float32)
    k = k.astype(jnp.float32)
    v = v.astype(jnp.float32)
    branches = [dilated_branch(q, k, v, w, d) for (w, d) in DILATED_PAIRS]
    m_all = branches[0][2]
    for _, _, m in branches[1:]:
        m_all = jnp.maximum(m_all, m)
    num = jnp.zeros_like(q)
    den = jnp.zeros_like(m_all)
    for n, dn, m in branches:
        scale = jnp.exp(m - m_all)
        num = num + n * scale[..., None]
        den = den + dn * scale
    return num / den[..., None]


def retention(q, k, v):
    Bsz, S, H, dk = q.shape
    dv = v.shape[-1]
    C = RET_CHUNK
    nc = S // C
    log_g = jnp.log1p(-(2.0 ** (-5.0 - jnp.arange(H, dtype=jnp.float32))))
    qc = q.astype(jnp.float32).reshape(Bsz, nc, C, H, dk)
    kc = k.astype(jnp.float32).reshape(Bsz, nc, C, H, dk) * (dk ** -0.5)
    vc = v.astype(jnp.float32).reshape(Bsz, nc, C, H, dv)
    i = jnp.arange(C, dtype=jnp.float32)
    diff = i[:, None] - i[None, :]
    dmat = jnp.where(diff[None] >= 0, jnp.exp(jnp.maximum(diff, 0.0)[None] * log_g[:, None, None]), 0.0)
    scores = jnp.einsum('bnihd,bnjhd->bnhij', qc, kc) * dmat
    y_intra = jnp.einsum('bnhij,bnjhe->bnihe', scores, vc)
    kdec = jnp.exp((C - 1 - i)[:, None] * log_g[None, :])
    kv = jnp.einsum('bnjhd,bnjhe->nbhde', kc * kdec[None, None, :, :, None], vc)
    chunk_decay = jnp.exp(C * log_g)[None, :, None, None]

    def step(state, kv_n):
        return state * chunk_decay + kv_n, state

    _, prev = lax.scan(step, jnp.zeros((Bsz, H, dk, dv), jnp.float32), kv)
    qdec = jnp.exp((i + 1)[:, None] * log_g[None, :])
    y_cross = jnp.einsum('bnihd,nbhde->bnihe', qc * qdec[None, None, :, :, None], prev)
    return (y_intra + y_cross).reshape(Bsz, S, H, dv)


def even_mixer(h, w_in, a_conv, a_conv_b, a_ln_g, a_ln_b, b_conv, w_out):
    a_val, a_gate, b_b, b_c, b_h = split_cols(h @ w_in, EVEN_SPLIT)
    a = a_val * jax.nn.sigmoid(a_gate)
    a = causal_dwconv(a, a_conv) + a_conv_b.astype(a.dtype)
    a = jax.nn.silu(layernorm(a, a_ln_g, a_ln_b))
    b = b_b * causal_dwconv(b_c * b_h, b_conv)
    return jnp.concatenate([a, b], axis=-1) @ w_out


def odd_mixer(h, w_in, w_out):
    Bsz, S, _ = h.shape
    cq, ck, cv, rq, rk, rv, rg = split_cols(h @ w_in, ODD_SPLIT)
    cq = rope(cq.reshape(Bsz, S, C_HEADS, HEAD_DIM)) * (HEAD_DIM ** -0.5)
    ck = rope(ck.reshape(Bsz, S, C_HEADS, HEAD_DIM))
    cv = cv.reshape(Bsz, S, C_HEADS, HEAD_DIM)
    c_out = dilated_attention(cq, ck, cv).reshape(Bsz, S, C_HEADS * HEAD_DIM)
    rq = rope(rq.reshape(Bsz, S, D_HEADS, HEAD_DIM))
    rk = rope(rk.reshape(Bsz, S, D_HEADS, HEAD_DIM))
    rv = rv.reshape(Bsz, S, D_HEADS, D_VDIM)
    y = head_norm(retention(rq, rk, rv)).reshape(Bsz, S, D_HEADS * D_VDIM)
    r_out = jax.nn.silu(rg.astype(jnp.float32)) * y
    mix = jnp.concatenate([c_out, r_out], axis=-1).astype(h.dtype)
    return mix @ w_out


def conv_ffn(h, w_up, conv_w, conv_b, w_down):
    u = causal_dwconv(h @ w_up, conv_w) + conv_b.astype(h.dtype)
    gate, up = jnp.split(u, 2, axis=-1)
    return (jax.nn.silu(gate) * up) @ w_down


def setup_inputs(seed: int = 0) -> dict:
    key = jax.random.key(seed)
    ks = jax.random.split(key, 18)
    f32 = jnp.float32

    def nrm(k, shape, scale):
        return jax.random.normal(k, shape, f32) * scale

    return {
        'x': jax.random.normal(ks[0], (BATCH, SEQ, D_MODEL), f32),
        'ev_norm': 1.0 + nrm(ks[1], (N_EVEN, D_MODEL), 0.02),
        'ev_w_in': nrm(ks[2], (N_EVEN, D_MODEL, EVEN_IN), D_MODEL ** -0.5),
        'ev_a_conv': nrm(ks[3], (N_EVEN, A_CONV, A_WIDTH), A_CONV ** -0.5),
        'ev_a_conv_b': nrm(ks[4], (N_EVEN, A_WIDTH), 0.02),
        'ev_a_ln_g': 1.0 + nrm(ks[5], (N_EVEN, A_WIDTH), 0.02),
        'ev_a_ln_b': nrm(ks[6], (N_EVEN, A_WIDTH), 0.02),
        'ev_b_conv': nrm(ks[7], (N_EVEN, B_CONV, B_WIDTH), B_CONV ** -0.5),
        'ev_w_out': nrm(ks[8], (N_EVEN, EVEN_MIX, D_MODEL), EVEN_MIX ** -0.5),
        'od_norm': 1.0 + nrm(ks[9], (N_ODD, D_MODEL), 0.02),
        'od_w_in': nrm(ks[10], (N_ODD, D_MODEL, ODD_IN), D_MODEL ** -0.5),
        'od_w_out': nrm(ks[11], (N_ODD, ODD_MIX, D_MODEL), ODD_MIX ** -0.5),
        'ffn_norm': 1.0 + nrm(ks[12], (DEPTH, D_MODEL), 0.02),
        'ffn_w_up': nrm(ks[13], (DEPTH, D_MODEL, 2 * D_FF), D_MODEL ** -0.5),
        'ffn_conv': nrm(ks[14], (DEPTH, FFN_CONV, 2 * D_FF), FFN_CONV ** -0.5),
        'ffn_conv_b': nrm(ks[15], (DEPTH, 2 * D_FF), 0.02),
        'ffn_w_down': nrm(ks[16], (DEPTH, D_FF, D_MODEL), D_FF ** -0.5),
        'final_norm': 1.0 + nrm(ks[17], (D_MODEL,), 0.02),
    }


def reference(x, ev_norm, ev_w_in, ev_a_conv, ev_a_conv_b, ev_a_ln_g, ev_a_ln_b, ev_b_conv, ev_w_out,
              od_norm, od_w_in, od_w_out, ffn_norm, ffn_w_up, ffn_conv, ffn_conv_b, ffn_w_down, final_norm):
    for layer in range(DEPTH):
        j = layer // 2
        if layer % 2 == 0:
            x = x + even_mixer(rmsnorm(x, ev_norm[j]), ev_w_in[j], ev_a_conv[j], ev_a_conv_b[j],
                               ev_a_ln_g[j], ev_a_ln_b[j], ev_b_conv[j], ev_w_out[j])
        else:
            x = x + odd_mixer(rmsnorm(x, od_norm[j]), od_w_in[j], od_w_out[j])
        x = x + conv_ffn(rmsnorm(x, ffn_norm[layer]), ffn_w_up[layer], ffn_conv[layer],
                         ffn_conv_b[layer], ffn_w_down[layer])
    return rmsnorm(x, final_norm)
```

```python
import functools

import jax
import jax.numpy as jnp
import numpy as np
from jax import lax
from jax.experimental import pallas as pl
from jax.experimental.pallas import tpu as pltpu

F32 = jnp.float32
BF16 = jnp.bfloat16

LANES = 128
SUBLANES = 8
VMEM_LIMIT_BYTES = 56 * 1024 * 1024

HEAD_DIM = 64
A_CONV = 31
SHORT_CONV = 3
DILATIONS = (1, 4, 16)
ATTN_BLOCK = 128
RET_CHUNK = 128
C_HEADS = 8
D_HEADS = 4
D_VDIM = 2 * HEAD_DIM
ROPE_THETA = 10000.0
NORM_EPS = 1e-6
NEG_BIG = -1e30

TOKEN_TILE = 512
A_HALO = 32
A_ROWS = 64
FF_CHUNK = 256


def _rmsnorm(x, g):
    return x * lax.rsqrt(jnp.mean(x * x, axis=-1, keepdims=True) + NORM_EPS) * g


def _dot(a, b):
    return jnp.dot(a, b, preferred_element_type=F32)


def _dot_nt(a, b):
    return lax.dot_general(a, b, (((1,), (1,)), ((), ())), preferred_element_type=F32)


def _dot_tn(a, b):
    return lax.dot_general(a, b, (((0,), (0,)), ((), ())), preferred_element_type=F32)


def _resident(a):
    return pl.BlockSpec(a.shape, lambda b, s: (0,) * a.ndim, pipeline_mode=pl.Buffered(1))


def _silu(x):
    return x * jax.nn.sigmoid(x)


def _shift_history(buf, halo, ts, first):
    @pl.when(first)
    def _():
        buf[0:halo, :] = jnp.zeros((halo, buf.shape[1]), buf.dtype)

    @pl.when(jnp.logical_not(first))
    def _():
        buf[0:halo, :] = buf[ts:ts + halo, :]


def _conv3(buf, w_ref, ts):
    o = SUBLANES
    return (buf[o:o + ts, :] * w_ref[2:3, :] + buf[o - 1:o - 1 + ts, :] * w_ref[1:2, :]
            + buf[o - 2:o - 2 + ts, :] * w_ref[0:1, :])


def _even_kernel(x_ref, g_ref, win_ref, aw_ref, ab_ref, lng_ref, lnb_ref, bw_ref, woa_ref, wob_ref, o_ref,
                 abuf, bbuf, mixa, *, ts, aw):
    first = pl.program_id(1) == 0
    x = x_ref[0]
    h = _rmsnorm(x, g_ref[...]).astype(BF16)
    p = _dot(h, win_ref[...])
    _shift_history(abuf, A_HALO, ts, first)
    _shift_history(bbuf, SUBLANES, ts, first)
    abuf[A_HALO:A_HALO + ts, :] = p[:, 0:aw] * jax.nn.sigmoid(p[:, aw:2 * aw])
    bbuf[SUBLANES:SUBLANES + ts, :] = p[:, 3 * aw:4 * aw] * p[:, 4 * aw:5 * aw]

    n_u = A_HALO // SUBLANES

    def conv_chunk(i, carry):
        r0 = pl.multiple_of(i * A_ROWS, A_ROWS)
        win = abuf[pl.ds(r0, A_ROWS + A_HALO), :]
        out = jnp.broadcast_to(ab_ref[...], (A_ROWS, aw))
        for v in range(SUBLANES):
            part = None
            for u in range(n_u):
                s = SUBLANES * u + v
                if s >= A_CONV:
                    continue
                lo = A_HALO - SUBLANES - SUBLANES * u
                term = win[lo:lo + A_ROWS + SUBLANES, :] * aw_ref[A_CONV - 1 - s:A_CONV - s, :]
                part = term if part is None else part + term
            out = out + part[SUBLANES - v:SUBLANES - v + A_ROWS, :]
        mu = jnp.mean(out, axis=-1, keepdims=True)
        oc = out - mu
        y = oc * lax.rsqrt(jnp.mean(oc * oc, axis=-1, keepdims=True) + NORM_EPS)
        y = y * lng_ref[...] + lnb_ref[...]
        mixa[pl.ds(r0, A_ROWS), :] = _silu(y).astype(BF16)
        return carry

    lax.fori_loop(0, ts // A_ROWS, conv_chunk, 0)

    b = p[:, 2 * aw:3 * aw] * _conv3(bbuf, bw_ref, ts)
    o_ref[0] = x + _dot(mixa[...], woa_ref[...]) + _dot(b.astype(BF16), wob_ref[...])


def _even_mixer(x, g, w_in, a_conv, a_conv_b, ln_g, ln_b, b_conv, w_out):
    bsz, seq, d = x.shape
    ts = TOKEN_TILE
    aw = a_conv.shape[1]
    assert seq % ts == 0 and ts % A_ROWS == 0 and A_HALO >= A_CONV - 1 and a_conv.shape[0] == A_CONV
    assert b_conv.shape[0] == SHORT_CONV and w_in.shape[1] == 5 * aw and w_out.shape[0] == 2 * aw
    full = _resident
    args = (g[None, :], w_in.astype(BF16), a_conv, a_conv_b[None, :], ln_g[None, :], ln_b[None, :], b_conv,
            w_out[:aw].astype(BF16), w_out[aw:].astype(BF16))
    tile = pl.BlockSpec((1, ts, d), lambda b, s: (b, s, 0))
    return pl.pallas_call(
        functools.partial(_even_kernel, ts=ts, aw=aw),
        grid=(bsz, seq // ts),
        in_specs=[tile] + [full(a) for a in args],
        out_specs=tile,
        out_shape=jax.ShapeDtypeStruct(x.shape, x.dtype),
        scratch_shapes=[pltpu.VMEM((ts + A_HALO, aw), F32), pltpu.VMEM((ts + SUBLANES, aw), F32),
                        pltpu.VMEM((ts, aw), BF16)],
        compiler_params=pltpu.CompilerParams(dimension_semantics=("arbitrary", "arbitrary"),
                                             vmem_limit_bytes=VMEM_LIMIT_BYTES),
        name="even_mixer",
    )(x, *args)


def _ffn_kernel(*refs, ts, n_chunks, ck, n_mix, final):
    x_ref = refs[0]
    mix_refs = refs[1:1 + 2 * n_mix]
    g_ref, wgu_ref, cw_ref, cb_ref, wdn_ref = refs[1 + 2 * n_mix:6 + 2 * n_mix]
    pos = 6 + 2 * n_mix
    fg_ref = refs[pos] if final else None
    pos += int(final)
    o_ref, hbuf, ubuf, hist, acc = refs[pos:pos + 5]
    first = pl.program_id(1) == 0

    x = x_ref[0]
    for j in range(n_mix):
        x = x + _dot(mix_refs[2 * j][0], mix_refs[2 * j + 1][...])
    hbuf[...] = _rmsnorm(x, g_ref[...]).astype(BF16)
    acc[...] = x

    @pl.when(first)
    def _():
        hist[...] = jnp.zeros(hist.shape, hist.dtype)

    def chunk(c, carry):
        ubuf[0:SUBLANES, :] = hist[c]
        ubuf[SUBLANES:SUBLANES + ts, :] = _dot(hbuf[...], wgu_ref[c])
        hist[c] = ubuf[ts:ts + SUBLANES, :]
        u = _conv3(ubuf, cw_ref.at[c], ts) + cb_ref[c]
        act = (_silu(u[:, :ck]) * u[:, ck:]).astype(BF16)
        acc[...] += _dot(act, wdn_ref[c])
        return carry

    lax.fori_loop(0, n_chunks, chunk, 0)
    y = acc[...]
    if final:
        y = _rmsnorm(y, fg_ref[...])
    o_ref[0] = y


def _conv_ffn(x, mixes, g, w_up, conv_w, conv_b, w_down, final_g):
    bsz, seq, d = x.shape
    ts = TOKEN_TILE
    d_ff = w_down.shape[0]
    ck = FF_CHUNK
    assert seq % ts == 0 and d_ff % ck == 0 and conv_w.shape[0] == SHORT_CONV
    nc = d_ff // ck

    def pair(a):
        r = a.shape[0]
        return jnp.concatenate([a[:, :d_ff].reshape(r, nc, ck), a[:, d_ff:].reshape(r, nc, ck)],
                               axis=-1).transpose(1, 0, 2)

    wgu = pair(w_up.astype(BF16))
    cw = jnp.pad(pair(conv_w), ((0, 0), (0, SUBLANES - SHORT_CONV), (0, 0)))
    cb = pair(conv_b[None, :])
    wdn = w_down.astype(BF16).reshape(nc, ck, d)

    full = _resident
    tile = pl.BlockSpec((1, ts, d), lambda b, s: (b, s, 0))
    args, specs = [x], [tile]
    for m, w in mixes:
        args += [m, w.astype(BF16)]
        specs += [pl.BlockSpec((1, ts, m.shape[-1]), lambda b, s: (b, s, 0)), full(w)]
    tail = [g[None, :], wgu, cw, cb, wdn] + ([final_g[None, :]] if final_g is not None else [])
    args += tail
    specs += [full(a) for a in tail]
    return pl.pallas_call(
        functools.partial(_ffn_kernel, ts=ts, n_chunks=nc, ck=ck, n_mix=len(mixes), final=final_g is not None),
        grid=(bsz, seq // ts),
        in_specs=specs,
        out_specs=tile,
        out_shape=jax.ShapeDtypeStruct(x.shape, x.dtype),
        scratch_shapes=[pltpu.VMEM((ts, d), BF16), pltpu.VMEM((ts + SUBLANES, 2 * ck), F32),
                        pltpu.VMEM((nc, SUBLANES, 2 * ck), F32), pltpu.VMEM((ts, d), F32)],
        compiler_params=pltpu.CompilerParams(dimension_semantics=("arbitrary", "arbitrary"),
                                             vmem_limit_bytes=VMEM_LIMIT_BYTES),
        name="conv_ffn",
    )(*args)


def _rope_tables(seq):
    half = HEAD_DIM // 2
    inv = ROPE_THETA ** (-jnp.arange(0, HEAD_DIM, 2, dtype=F32) / HEAD_DIM)
    ang = jnp.arange(seq, dtype=F32)[:, None] * inv[None, :]
    cos, sin = jnp.cos(ang), jnp.sin(ang)
    reps = LANES // HEAD_DIM
    return jnp.tile(jnp.concatenate([cos, cos], -1), (1, reps)), jnp.tile(jnp.concatenate([-sin, sin], -1), (1, reps))


def _odd_proj_kernel(x_ref, g_ref, win_ref, cos_ref, sin_ref, cq_ref, ck_ref, cv_ref, rq_ref, rk_ref, rv_ref,
                     rg_ref, *, cw, rw, rvw):
    h = _rmsnorm(x_ref[0], g_ref[...]).astype(BF16)
    p = _dot(h, win_ref[...])
    cos, sin = cos_ref[...], sin_ref[...]
    half = HEAD_DIM // 2
    first_half = (lax.broadcasted_iota(jnp.int32, cos.shape, 1) % HEAD_DIM) < half

    def rope_to(dst, col0, width, scale):
        for j in range(width // LANES):
            xb = p[:, col0 + j * LANES:col0 + (j + 1) * LANES]
            partner = jnp.where(first_half, pltpu.roll(xb, LANES - half, 1), pltpu.roll(xb, half, 1))
            y = xb * cos + partner * sin
            if scale != 1.0:
                y = y * scale
            dst[0, :, j * LANES:(j + 1) * LANES] = y.astype(dst.dtype)

    qk_scale = HEAD_DIM ** -0.5
    rope_to(cq_ref, 0, cw, qk_scale)
    rope_to(ck_ref, cw, cw, 1.0)
    cv_ref[0] = p[:, 2 * cw:3 * cw]
    rope_to(rq_ref, 3 * cw, rw, 1.0)
    rope_to(rk_ref, 3 * cw + rw, rw, qk_scale)
    rv_ref[0] = p[:, 3 * cw + 2 * rw:3 * cw + 2 * rw + rvw].astype(rv_ref.dtype)
    rg_ref[0] = p[:, 3 * cw + 2 * rw + rvw:].astype(rg_ref.dtype)


def _odd_proj(x, g, w_in):
    bsz, seq, d = x.shape
    ts = TOKEN_TILE
    cw, rw, rvw = C_HEADS * HEAD_DIM, D_HEADS * HEAD_DIM, D_HEADS * D_VDIM
    assert w_in.shape[1] == 3 * cw + 2 * rw + 2 * rvw and seq % ts == 0
    cos, sin = _rope_tables(seq)
    full = _resident
    tile = lambda w: pl.BlockSpec((1, ts, w), lambda b, s: (b, s, 0))
    tab = pl.BlockSpec((ts, LANES), lambda b, s: (s, 0))
    g2, w = g[None, :], w_in.astype(BF16)
    widths = (cw, cw, cw, rw, rw, rvw, rvw)
    dtypes = (F32, F32, F32, BF16, BF16, BF16, BF16)
    return pl.pallas_call(
        functools.partial(_odd_proj_kernel, cw=cw, rw=rw, rvw=rvw),
        grid=(bsz, seq // ts),
        in_specs=[tile(d), full(g2), full(w), tab, tab],
        out_specs=[tile(wd) for wd in widths],
        out_shape=[jax.ShapeDtypeStruct((bsz, seq, wd), dt) for wd, dt in zip(widths, dtypes)],
        compiler_params=pltpu.CompilerParams(dimension_semantics=("arbitrary", "arbitrary"),
                                             vmem_limit_bytes=VMEM_LIMIT_BYTES),
        name="odd_proj",
    )(x, g2, w, cos, sin)


def _attn_kernel(q_ref, k_ref, v_ref, o_ref, m_ref, l_ref, acc_ref, *, seq):
    blk = ATTN_BLOCK
    lane = lax.broadcasted_iota(jnp.int32, (blk, LANES), 1)
    head0 = lane < HEAD_DIM
    qi = lax.broadcasted_iota(jnp.int32, (blk, 2 * blk), 0)
    ki = lax.broadcasted_iota(jnp.int32, (blk, 2 * blk), 1)
    valid_both = ((ki < blk) & (ki >= qi)) | ((ki >= blk) & (ki - blk <= qi))
    valid_cur = (lax.broadcasted_iota(jnp.int32, (blk, blk), 1) <= lax.broadcasted_iota(jnp.int32, (blk, blk), 0))

    def rows(ref, start, n, d):
        if d == 1:
            return ref[0, pl.ds(start, n), :]
        return ref[0, pl.ds(start, n, stride=d), :]

    def put(ref, idx, start, d, val):
        if d == 1:
            ref[idx, pl.ds(start, blk), :] = val
        else:
            ref[idx, pl.ds(start, blk, stride=d), :] = val

    def block(q_start, k_start, d, with_prev, init):
        q = rows(q_ref, q_start, blk, d)
        nk = 2 * blk if with_prev else blk
        k = rows(k_ref, k_start, nk, d).astype(BF16)
        v = rows(v_ref, k_start, nk, d).astype(BF16)
        valid = valid_both if with_prev else valid_cur
        acc_new = None
        alphas = []
        for hd in range(2):
            qh = jnp.where(head0 if hd == 0 else jnp.logical_not(head0), q, 0.0).astype(BF16)
            s = jnp.where(valid, _dot_nt(qh, k), NEG_BIG)
            m_blk = jnp.max(s, axis=-1, keepdims=True)
            if init:
                m_new = jnp.broadcast_to(m_blk, (blk, LANES))
            else:
                m_old = rows(m_ref.at[hd:hd + 1], q_start, blk, d)
                m_new = jnp.maximum(m_old, m_blk)
            p = jnp.exp(s - (jnp.concatenate([m_new, m_new], axis=1) if with_prev else m_new))
            l_blk = jnp.sum(p, axis=-1, keepdims=True)
            pv = _dot(p.astype(BF16), v)
            if init:
                l_new = jnp.broadcast_to(l_blk, (blk, LANES))
            else:
                alpha = jnp.exp(m_old - m_new)
                alphas.append(alpha)
                l_new = rows(l_ref.at[hd:hd + 1], q_start, blk, d) * alpha + l_blk
            put(m_ref, hd, q_start, d, m_new)
            put(l_ref, hd, q_start, d, l_new)
            acc_new = pv if acc_new is None else jnp.where(head0, acc_new, pv)
        if not init:
            acc_new = rows(acc_ref.at[0:1], q_start, blk, d) * jnp.where(head0, alphas[0], alphas[1]) + acc_new
        put(acc_ref, 0, q_start, d, acc_new)

    for d in DILATIONS:
        n_blocks = seq // (d * blk)
        init = d == DILATIONS[0]

        def sub_sequence(r, carry, d=d, n_blocks=n_blocks, init=init):
            block(r, r, d, False, init)

            def body(n, c):
                q_start = r + n * (d * blk)
                if d == 1:
                    q_start = pl.multiple_of(q_start, blk)
                block(q_start, q_start - d * blk, d, True, init)
                return c

            lax.fori_loop(1, n_blocks, body, 0)
            return carry

        if d == 1:
            sub_sequence(0, 0)
        else:
            lax.fori_loop(0, d, sub_sequence, 0)

    def finish(i, carry):
        r0 = pl.multiple_of(i * blk, blk)
        l = jnp.where(head0, l_ref[0, pl.ds(r0, blk), :], l_ref[1, pl.ds(r0, blk), :])
        o_ref[0, pl.ds(r0, blk), :] = (acc_ref[0, pl.ds(r0, blk), :] / l).astype(o_ref.dtype)
        return carry

    lax.fori_loop(0, seq // blk, finish, 0)


def _dilated_attention(cq, ck, cv):
    bsz, seq, cw = cq.shape
    assert cw % LANES == 0 and LANES == 2 * HEAD_DIM
    for d in DILATIONS:
        assert seq % (d * ATTN_BLOCK) == 0
    spec = pl.BlockSpec((1, seq, LANES), lambda b, h: (b, 0, h))
    return pl.pallas_call(
        functools.partial(_attn_kernel, seq=seq),
        grid=(bsz, cw // LANES),
        in_specs=[spec, spec, spec],
        out_specs=spec,
        out_shape=jax.ShapeDtypeStruct((bsz, seq, cw), BF16),
        scratch_shapes=[pltpu.VMEM((2, seq, LANES), F32), pltpu.VMEM((2, seq, LANES), F32),
                        pltpu.VMEM((1, seq, LANES), F32)],
        compiler_params=pltpu.CompilerParams(dimension_semantics=("arbitrary", "arbitrary"),
                                             vmem_limit_bytes=VMEM_LIMIT_BYTES),
        name="dilated_attention",
    )(cq, ck, cv)


def _retention_tables():
    c = RET_CHUNK
    log_g = jnp.log1p(-(2.0 ** (-5.0 - jnp.arange(D_HEADS, dtype=F32))))
    i = jnp.arange(c, dtype=F32)
    diff = i[:, None] - i[None, :]
    dmat = jnp.where(diff[None] >= 0, jnp.exp(jnp.maximum(diff, 0.0)[None] * log_g[:, None, None]), 0.0)
    kdec = jnp.exp((c - 1 - i)[None, :] * log_g[:, None])
    qdec = jnp.exp((i + 1)[None, :] * log_g[:, None])
    cdec = jnp.exp(c * log_g)
    rep = lambda t: jnp.broadcast_to(t[:, :, None], (D_HEADS, c, LANES))
    return dmat, rep(kdec), rep(qdec), jnp.broadcast_to(cdec[:, None, None], (D_HEADS, SUBLANES, LANES))


def _retention_kernel(q_ref, k_ref, v_ref, g_ref, dmat_ref, kdec_ref, qdec_ref, cdec_ref, o_ref, state, *, seq):
    c = RET_CHUNK
    lane = lax.broadcasted_iota(jnp.int32, (c, LANES), 1)
    head0 = lane < HEAD_DIM
    state[...] = jnp.zeros(state.shape, state.dtype)

    def chunk(n, carry):
        r0 = pl.multiple_of(n * c, c)
        q = q_ref[0, pl.ds(r0, c), :].astype(F32)
        k = k_ref[0, pl.ds(r0, c), :].astype(F32)
        for hd in range(2):
            sel = head0 if hd == 0 else jnp.logical_not(head0)
            qh = jnp.where(sel, q, 0.0)
            kh = jnp.where(sel, k, 0.0)
            v = v_ref[0, pl.ds(r0, c), hd * D_VDIM:(hd + 1) * D_VDIM]
            scores = _dot_nt(qh.astype(BF16), kh.astype(BF16)) * dmat_ref[hd]
            y = _dot(scores.astype(BF16), v)
            y = y + _dot((qh * qdec_ref[hd]).astype(BF16), state[hd].astype(BF16))
            kv = _dot_tn((kh * kdec_ref[hd]).astype(BF16), v)
            state[hd] = state[hd] * cdec_ref[hd, 0:1, :] + kv
            mu = jnp.mean(y, axis=-1, keepdims=True)
            yc = y - mu
            yn = yc * lax.rsqrt(jnp.mean(yc * yc, axis=-1, keepdims=True) + NORM_EPS)
            gate = g_ref[0, pl.ds(r0, c), hd * D_VDIM:(hd + 1) * D_VDIM].astype(F32)
            o_ref[0, pl.ds(r0, c), hd * D_VDIM:(hd + 1) * D_VDIM] = (_silu(gate) * yn).astype(o_ref.dtype)
        return carry

    lax.fori_loop(0, seq // c, chunk, 0)


def _retention(rq, rk, rv, rg):
    bsz, seq, rw = rq.shape
    assert rw % LANES == 0 and seq % RET_CHUNK == 0 and D_VDIM == LANES
    tabs = _retention_tables()
    n_pairs = rw // LANES
    qk_spec = pl.BlockSpec((1, seq, LANES), lambda b, h: (b, 0, h))
    v_spec = pl.BlockSpec((1, seq, 2 * D_VDIM), lambda b, h: (b, 0, h))
    tab_spec = lambda t: pl.BlockSpec((2,) + t.shape[1:], lambda b, h: (h, 0, 0))
    return pl.pallas_call(
        functools.partial(_retention_kernel, seq=seq),
        grid=(bsz, n_pairs),
        in_specs=[qk_spec, qk_spec, v_spec, v_spec] + [tab_spec(t) for t in tabs],
        out_specs=v_spec,
        out_shape=jax.ShapeDtypeStruct(rv.shape, BF16),
        scratch_shapes=[pltpu.VMEM((2, LANES, D_VDIM), F32)],
        compiler_params=pltpu.CompilerParams(dimension_semantics=("arbitrary", "arbitrary"),
                                             vmem_limit_bytes=VMEM_LIMIT_BYTES),
        name="retention",
    )(rq, rk, rv, rg, *tabs)


def kernel(x, ev_norm, ev_w_in, ev_a_conv, ev_a_conv_b, ev_a_ln_g, ev_a_ln_b, ev_b_conv, ev_w_out, od_norm, od_w_in, od_w_out, ffn_norm, ffn_w_up, ffn_conv, ffn_conv_b, ffn_w_down, final_norm):
    depth = ffn_norm.shape[0]
    c_width = C_HEADS * HEAD_DIM
    for layer in range(depth):
        j = layer // 2
        mixes = []
        if layer % 2 == 0:
            x = _even_mixer(x, ev_norm[j], ev_w_in[j], ev_a_conv[j], ev_a_conv_b[j], ev_a_ln_g[j], ev_a_ln_b[j],
                            ev_b_conv[j], ev_w_out[j])
        else:
            cq, ck, cv, rq, rk, rv, rg = _odd_proj(x, od_norm[j], od_w_in[j])
            mixes = [(_dilated_attention(cq, ck, cv), od_w_out[j][:c_width]),
                     (_retention(rq, rk, rv, rg), od_w_out[j][c_width:])]
        x = _conv_ffn(x, mixes, ffn_norm[layer], ffn_w_up[layer], ffn_conv[layer], ffn_conv_b[layer],
                      ffn_w_down[layer], final_norm if layer == depth - 1 else None)
    return x
```

```python
import functools

import jax
import jax.numpy as jnp
import numpy as np
from jax import lax
from jax.experimental import pallas as pl
from jax.experimental.pallas import tpu as pltpu

F32 = jnp.float32
BF16 = jnp.bfloat16

LANES = 128
SUBLANES = 8
VMEM_LIMIT_BYTES = 56 * 1024 * 1024

HEAD_DIM = 64
A_CONV = 31
SHORT_CONV = 3
DILATIONS = (1, 4, 16)
ATTN_BLOCK = 128
ATTN_INTERLEAVE = 8
RET_CHUNK = 128
RET_INTERLEAVE = 4
C_HEADS = 8
D_HEADS = 4
D_VDIM = 2 * HEAD_DIM
ROPE_THETA = 10000.0
NORM_EPS = 1e-6
NEG_BIG = -1e30
LOG2_E = 1.4426950408889634

TOKEN_TILE = 512
A_HALO = 32
A_ROWS = 64
FF_CHUNK = 256


def _rmsnorm(x, g):
    return x * lax.rsqrt(jnp.mean(x * x, axis=-1, keepdims=True) + NORM_EPS) * g


def _dot(a, b):
    return jnp.dot(a, b, preferred_element_type=F32)


def _dot_nt(a, b):
    return lax.dot_general(a, b, (((1,), (1,)), ((), ())), preferred_element_type=F32)


def _dot_tn(a, b):
    return lax.dot_general(a, b, (((0,), (0,)), ((), ())), preferred_element_type=F32)


def _resident(a):
    return pl.BlockSpec(a.shape, lambda b, s: (0,) * a.ndim, pipeline_mode=pl.Buffered(1))


def _silu(x):
    return x * jax.nn.sigmoid(x)


def _shift_history(buf, halo, ts, first):
    @pl.when(first)
    def _():
        buf[0:halo, :] = jnp.zeros((halo, buf.shape[1]), buf.dtype)

    @pl.when(jnp.logical_not(first))
    def _():
        buf[0:halo, :] = buf[ts:ts + halo, :]


def _conv3(buf, w_ref, ts):
    o = SUBLANES
    return (buf[o:o + ts, :] * w_ref[2:3, :] + buf[o - 1:o - 1 + ts, :] * w_ref[1:2, :]
            + buf[o - 2:o - 2 + ts, :] * w_ref[0:1, :])


def _even_kernel(x_ref, g_ref, win_ref, aw_ref, ab_ref, lng_ref, lnb_ref, bw_ref, woa_ref, wob_ref, o_ref,
                 abuf, bbuf, mixa, *, ts, aw):
    first = pl.program_id(1) == 0
    x = x_ref[0]
    h = _rmsnorm(x, g_ref[...]).astype(BF16)
    p = _dot(h, win_ref[...])
    _shift_history(abuf, A_HALO, ts, first)
    _shift_history(bbuf, SUBLANES, ts, first)
    abuf[A_HALO:A_HALO + ts, :] = p[:, 0:aw] * jax.nn.sigmoid(p[:, aw:2 * aw])
    bbuf[SUBLANES:SUBLANES + ts, :] = p[:, 3 * aw:4 * aw] * p[:, 4 * aw:5 * aw]

    n_u = A_HALO // SUBLANES

    def conv_chunk(i, carry):
        r0 = pl.multiple_of(i * A_ROWS, A_ROWS)
        win = abuf[pl.ds(r0, A_ROWS + A_HALO), :]
        out = jnp.broadcast_to(ab_ref[...], (A_ROWS, aw))
        for v in range(SUBLANES):
            part = None
            for u in range(n_u):
                s = SUBLANES * u + v
                if s >= A_CONV:
                    continue
                lo = A_HALO - SUBLANES - SUBLANES * u
                term = win[lo:lo + A_ROWS + SUBLANES, :] * aw_ref[A_CONV - 1 - s:A_CONV - s, :]
                part = term if part is None else part + term
            out = out + part[SUBLANES - v:SUBLANES - v + A_ROWS, :]
        mu = jnp.mean(out, axis=-1, keepdims=True)
        oc = out - mu
        y = oc * lax.rsqrt(jnp.mean(oc * oc, axis=-1, keepdims=True) + NORM_EPS)
        y = y * lng_ref[...] + lnb_ref[...]
        mixa[pl.ds(r0, A_ROWS), :] = _silu(y).astype(BF16)
        return carry

    lax.fori_loop(0, ts // A_ROWS, conv_chunk, 0)

    b = p[:, 2 * aw:3 * aw] * _conv3(bbuf, bw_ref, ts)
    o_ref[0] = x + _dot(mixa[...], woa_ref[...]) + _dot(b.astype(BF16), wob_ref[...])


def _even_mixer(x, g, w_in, a_conv, a_conv_b, ln_g, ln_b, b_conv, w_out):
    bsz, seq, d = x.shape
    ts = TOKEN_TILE
    aw = a_conv.shape[1]
    assert seq % ts == 0 and ts % A_ROWS == 0 and A_HALO >= A_CONV - 1 and a_conv.shape[0] == A_CONV
    assert b_conv.shape[0] == SHORT_CONV and w_in.shape[1] == 5 * aw and w_out.shape[0] == 2 * aw
    full = _resident
    args = (g[None, :], w_in.astype(BF16), a_conv, a_conv_b[None, :], ln_g[None, :], ln_b[None, :], b_conv,
            w_out[:aw].astype(BF16), w_out[aw:].astype(BF16))
    tile = pl.BlockSpec((1, ts, d), lambda b, s: (b, s, 0))
    return pl.pallas_call(
        functools.partial(_even_kernel, ts=ts, aw=aw),
        grid=(bsz, seq // ts),
        in_specs=[tile] + [full(a) for a in args],
        out_specs=tile,
        out_shape=jax.ShapeDtypeStruct(x.shape, x.dtype),
        scratch_shapes=[pltpu.VMEM((ts + A_HALO, aw), F32), pltpu.VMEM((ts + SUBLANES, aw), F32),
                        pltpu.VMEM((ts, aw), BF16)],
        compiler_params=pltpu.CompilerParams(dimension_semantics=("arbitrary", "arbitrary"),
                                             vmem_limit_bytes=VMEM_LIMIT_BYTES),
        name="even_mixer",
    )(x, *args)


def _ffn_kernel(*refs, ts, n_chunks, ck, n_mix, final):
    x_ref = refs[0]
    mix_refs = refs[1:1 + 2 * n_mix]
    g_ref, wgu_ref, cw_ref, cb_ref, wdn_ref = refs[1 + 2 * n_mix:6 + 2 * n_mix]
    pos = 6 + 2 * n_mix
    fg_ref = refs[pos] if final else None
    pos += int(final)
    o_ref, hbuf, ubuf, hist, acc = refs[pos:pos + 5]
    first = pl.program_id(1) == 0

    x = x_ref[0]
    for j in range(n_mix):
        x = x + _dot(mix_refs[2 * j][0], mix_refs[2 * j + 1][...])
    hbuf[...] = _rmsnorm(x, g_ref[...]).astype(BF16)
    acc[...] = x

    @pl.when(first)
    def _():
        hist[...] = jnp.zeros(hist.shape, hist.dtype)

    def up(c, buf):
        buf[0:SUBLANES, :] = hist[c]
        buf[SUBLANES:SUBLANES + ts, :] = _dot(hbuf[...], wgu_ref[c])
        hist[c] = buf[ts:ts + SUBLANES, :]

    def down(c, buf):
        u = _conv3(buf, cw_ref.at[c], ts) + cb_ref[c]
        act = (_silu(u[:, :ck]) * u[:, ck:]).astype(BF16)
        acc[...] += _dot(act, wdn_ref[c])

    up(0, ubuf.at[0])

    def chunk_pair(i, carry):
        c = 2 * i
        up(c + 1, ubuf.at[1])
        down(c, ubuf.at[0])
        up(c + 2, ubuf.at[0])
        down(c + 1, ubuf.at[1])
        return carry

    lax.fori_loop(0, (n_chunks - 1) // 2, chunk_pair, 0)
    down(n_chunks - 1, ubuf.at[0])
    y = acc[...]
    if final:
        y = _rmsnorm(y, fg_ref[...])
    o_ref[0] = y


def _conv_ffn(x, mixes, g, w_up, conv_w, conv_b, w_down, final_g):
    bsz, seq, d = x.shape
    ts = TOKEN_TILE
    d_ff = w_down.shape[0]
    ck = FF_CHUNK
    assert seq % ts == 0 and d_ff % ck == 0 and conv_w.shape[0] == SHORT_CONV
    nc = d_ff // ck
    assert nc % 2 == 1

    def pair(a):
        r = a.shape[0]
        return jnp.concatenate([a[:, :d_ff].reshape(r, nc, ck), a[:, d_ff:].reshape(r, nc, ck)],
                               axis=-1).transpose(1, 0, 2)

    wgu = pair(w_up.astype(BF16))
    cw = jnp.pad(pair(conv_w), ((0, 0), (0, SUBLANES - SHORT_CONV), (0, 0)))
    cb = pair(conv_b[None, :])
    wdn = w_down.astype(BF16).reshape(nc, ck, d)

    full = _resident
    tile = pl.BlockSpec((1, ts, d), lambda b, s: (b, s, 0))
    args, specs = [x], [tile]
    for m, w in mixes:
        args += [m, w.astype(BF16)]
        specs += [pl.BlockSpec((1, ts, m.shape[-1]), lambda b, s: (b, s, 0)), full(w)]
    tail = [g[None, :], wgu, cw, cb, wdn] + ([final_g[None, :]] if final_g is not None else [])
    args += tail
    specs += [full(a) for a in tail]
    return pl.pallas_call(
        functools.partial(_ffn_kernel, ts=ts, n_chunks=nc, ck=ck, n_mix=len(mixes), final=final_g is not None),
        grid=(bsz, seq // ts),
        in_specs=specs,
        out_specs=tile,
        out_shape=jax.ShapeDtypeStruct(x.shape, x.dtype),
        scratch_shapes=[pltpu.VMEM((ts, d), BF16), pltpu.VMEM((2, ts + SUBLANES, 2 * ck), F32),
                        pltpu.VMEM((nc, SUBLANES, 2 * ck), F32), pltpu.VMEM((ts, d), F32)],
        compiler_params=pltpu.CompilerParams(dimension_semantics=("arbitrary", "arbitrary"),
                                             vmem_limit_bytes=VMEM_LIMIT_BYTES),
        name="conv_ffn",
    )(*args)


def _rope_tables(seq):
    half = HEAD_DIM // 2
    inv = ROPE_THETA ** (-jnp.arange(0, HEAD_DIM, 2, dtype=F32) / HEAD_DIM)
    ang = jnp.arange(seq, dtype=F32)[:, None] * inv[None, :]
    cos, sin = jnp.cos(ang), jnp.sin(ang)
    reps = LANES // HEAD_DIM
    return jnp.tile(jnp.concatenate([cos, cos], -1), (1, reps)), jnp.tile(jnp.concatenate([-sin, sin], -1), (1, reps))


def _odd_proj_kernel(x_ref, g_ref, win_ref, cos_ref, sin_ref, cq_ref, ck_ref, cv_ref, rq_ref, rk_ref, rv_ref,
                     rg_ref, *, cw, rw, rvw):
    h = _rmsnorm(x_ref[0], g_ref[...]).astype(BF16)
    p = _dot(h, win_ref[...])
    cos, sin = cos_ref[...], sin_ref[...]
    half = HEAD_DIM // 2
    first_half = (lax.broadcasted_iota(jnp.int32, cos.shape, 1) % HEAD_DIM) < half

    def rope_to(dst, col0, width, scale):
        for j in range(width // LANES):
            xb = p[:, col0 + j * LANES:col0 + (j + 1) * LANES]
            partner = jnp.where(first_half, pltpu.roll(xb, LANES - half, 1), pltpu.roll(xb, half, 1))
            y = xb * cos + partner * sin
            if scale != 1.0:
                y = y * scale
            dst[0, :, j * LANES:(j + 1) * LANES] = y.astype(dst.dtype)

    qk_scale = HEAD_DIM ** -0.5
    rope_to(cq_ref, 0, cw, qk_scale)
    rope_to(ck_ref, cw, cw, 1.0)
    cv_ref[0] = p[:, 2 * cw:3 * cw]
    rope_to(rq_ref, 3 * cw, rw, 1.0)
    rope_to(rk_ref, 3 * cw + rw, rw, qk_scale)
    rv_ref[0] = p[:, 3 * cw + 2 * rw:3 * cw + 2 * rw + rvw].astype(rv_ref.dtype)
    rg_ref[0] = p[:, 3 * cw + 2 * rw + rvw:].astype(rg_ref.dtype)


def _odd_proj(x, g, w_in):
    bsz, seq, d = x.shape
    ts = TOKEN_TILE
    cw, rw, rvw = C_HEADS * HEAD_DIM, D_HEADS * HEAD_DIM, D_HEADS * D_VDIM
    assert w_in.shape[1] == 3 * cw + 2 * rw + 2 * rvw and seq % ts == 0
    cos, sin = _rope_tables(seq)
    full = _resident
    tile = lambda w: pl.BlockSpec((1, ts, w), lambda b, s: (b, s, 0))
    tab = pl.BlockSpec((ts, LANES), lambda b, s: (s, 0))
    g2, w = g[None, :], w_in.astype(BF16)
    widths = (cw, cw, cw, rw, rw, rvw, rvw)
    dtypes = (F32, F32, F32, BF16, BF16, BF16, BF16)
    return pl.pallas_call(
        functools.partial(_odd_proj_kernel, cw=cw, rw=rw, rvw=rvw),
        grid=(bsz, seq // ts),
        in_specs=[tile(d), full(g2), full(w), tab, tab],
        out_specs=[tile(wd) for wd in widths],
        out_shape=[jax.ShapeDtypeStruct((bsz, seq, wd), dt) for wd, dt in zip(widths, dtypes)],
        compiler_params=pltpu.CompilerParams(dimension_semantics=("arbitrary", "arbitrary"),
                                             vmem_limit_bytes=VMEM_LIMIT_BYTES),
        name="odd_proj",
    )(x, g2, w, cos, sin)


def _attn_kernel(q_ref, k_ref, v_ref, o_ref, num_ref, m_ref, l_ref, *, seq):
    blk = ATTN_BLOCK
    lane = lax.broadcasted_iota(jnp.int32, (blk, LANES), 1)
    head0 = lane < HEAD_DIM
    qi = lax.broadcasted_iota(jnp.int32, (2 * blk, 2 * blk), 0) & (blk - 1)
    ki = lax.broadcasted_iota(jnp.int32, (2 * blk, 2 * blk), 1)
    valid_both = ((ki < blk) & (ki >= qi)) | ((ki >= blk) & (ki - blk <= qi))
    valid_cur = (lax.broadcasted_iota(jnp.int32, (2 * blk, blk), 1)
                 <= (lax.broadcasted_iota(jnp.int32, (2 * blk, blk), 0) & (blk - 1)))

    def rows(ref, start, n, d):
        if d == 1:
            return ref[0, pl.ds(start, n), :]
        return ref[0, pl.ds(start, n, stride=d), :]

    def put(ref, idx, start, d, val):
        if d == 1:
            ref[idx, pl.ds(start, blk), :] = val
        else:
            ref[idx, pl.ds(start, blk, stride=d), :] = val

    def block(branch, q_start, d, with_prev):
        k_start = q_start - d * blk if with_prev else q_start
        nk = 2 * blk if with_prev else blk
        q = rows(q_ref, q_start, blk, d) * LOG2_E
        k = rows(k_ref, k_start, nk, d).astype(BF16)
        v = rows(v_ref, k_start, nk, d).astype(BF16)
        q2 = jnp.concatenate([jnp.where(head0, q, 0.0), jnp.where(head0, 0.0, q)], axis=0).astype(BF16)
        s = jnp.where(valid_both if with_prev else valid_cur, _dot_nt(q2, k), NEG_BIG)
        m = jnp.max(s, axis=-1, keepdims=True)
        p = jnp.exp2(s - m)
        l = jnp.sum(p, axis=-1, keepdims=True)
        pv = _dot(p.astype(BF16), v)
        put(num_ref, branch, q_start, d, jnp.where(head0, pv[:blk], pv[blk:]))
        put(m_ref, branch, q_start, d, jnp.where(head0, m[:blk], m[blk:]))
        put(l_ref, branch, q_start, d, jnp.where(head0, l[:blk], l[blk:]))

    for branch, d in enumerate(DILATIONS):
        n_blocks = seq // (d * blk)
        step = d * blk
        subs = min(d, ATTN_INTERLEAVE)
        split = ATTN_INTERLEAVE // subs
        span = n_blocks // split

        def issue(r0, n, first, branch=branch, d=d, step=step, subs=subs, split=split, span=span):
            for r in range(subs):
                for j in range(split):
                    start = r0 + r + (n + j * span) * step
                    if d == 1 and not isinstance(start, int):
                        start = pl.multiple_of(start, blk)
                    block(branch, start, d, not (first and j == 0))

        def sub_group(g, carry, issue=issue, subs=subs, span=span):
            r0 = g * subs
            issue(r0, 0, True)

            def body(n, c):
                issue(r0, n, False)
                return c

            lax.fori_loop(1, span, body, 0)
            return carry

        if d == subs:
            sub_group(0, 0)
        else:
            lax.fori_loop(0, d // subs, sub_group, 0)

    def finish(i, carry):
        sl = pl.ds(pl.multiple_of(i * blk, blk), blk)
        ms = [m_ref[b, sl, :] for b in range(len(DILATIONS))]
        m_all = functools.reduce(jnp.maximum, ms)
        num = jnp.zeros((blk, LANES), F32)
        den = jnp.zeros((blk, LANES), F32)
        for b, m in enumerate(ms):
            scale = jnp.exp2(m - m_all)
            num = num + num_ref[b, sl, :] * scale
            den = den + l_ref[b, sl, :] * scale
        o_ref[0, sl, :] = (num / den).astype(o_ref.dtype)
        return carry

    lax.fori_loop(0, seq // blk, finish, 0)


def _dilated_attention(cq, ck, cv):
    bsz, seq, cw = cq.shape
    assert cw % LANES == 0 and LANES == 2 * HEAD_DIM
    for d in DILATIONS:
        assert seq % (d * ATTN_BLOCK) == 0
    spec = pl.BlockSpec((1, seq, LANES), lambda b, h: (b, 0, h))
    return pl.pallas_call(
        functools.partial(_attn_kernel, seq=seq),
        grid=(bsz, cw // LANES),
        in_specs=[spec, spec, spec],
        out_specs=spec,
        out_shape=jax.ShapeDtypeStruct((bsz, seq, cw), BF16),
        scratch_shapes=[pltpu.VMEM((len(DILATIONS), seq, LANES), F32)] * 3,
        compiler_params=pltpu.CompilerParams(dimension_semantics=("arbitrary", "arbitrary"),
                                             vmem_limit_bytes=VMEM_LIMIT_BYTES),
        name="dilated_attention",
    )(cq, ck, cv)


def _retention_tables():
    c = RET_CHUNK
    log_g = jnp.log1p(-(2.0 ** (-5.0 - jnp.arange(D_HEADS, dtype=F32))))
    i = jnp.arange(c, dtype=F32)
    diff = i[:, None] - i[None, :]
    dmat = jnp.where(diff[None] >= 0, jnp.exp(jnp.maximum(diff, 0.0)[None] * log_g[:, None, None]), 0.0)
    kdec = jnp.exp((c - 1 - i)[None, :] * log_g[:, None])
    qdec = jnp.exp((i + 1)[None, :] * log_g[:, None])
    cdec = jnp.exp(c * log_g)
    rep = lambda t: jnp.broadcast_to(t[:, :, None], (D_HEADS, c, LANES))
    return dmat, rep(kdec), rep(qdec), jnp.broadcast_to(cdec[:, None, None], (D_HEADS, SUBLANES, LANES))


def _retention_kernel(q_ref, k_ref, v_ref, g_ref, dmat_ref, kdec_ref, qdec_ref, cdec_ref, o_ref, *, seq):
    c = RET_CHUNK
    lane = lax.broadcasted_iota(jnp.int32, (c, LANES), 1)
    head0 = lane < HEAD_DIM

    def chunk(r0, states):
        q = q_ref[0, pl.ds(r0, c), :].astype(F32)
        k = k_ref[0, pl.ds(r0, c), :].astype(F32)
        new_states = []
        for hd in range(2):
            sel = head0 if hd == 0 else jnp.logical_not(head0)
            qh = jnp.where(sel, q, 0.0)
            kh = jnp.where(sel, k, 0.0)
            v = v_ref[0, pl.ds(r0, c), hd * D_VDIM:(hd + 1) * D_VDIM]
            scores = _dot_nt(qh.astype(BF16), kh.astype(BF16)) * dmat_ref[hd]
            y = _dot(scores.astype(BF16), v)
            y = y + _dot((qh * qdec_ref[hd]).astype(BF16), states[hd].astype(BF16))
            kv = _dot_tn((kh * kdec_ref[hd]).astype(BF16), v)
            new_states.append(states[hd] * cdec_ref[hd, 0:1, :] + kv)
            mu = jnp.mean(y, axis=-1, keepdims=True)
            yc = y - mu
            yn = yc * lax.rsqrt(jnp.mean(yc * yc, axis=-1, keepdims=True) + NORM_EPS)
            gate = g_ref[0, pl.ds(r0, c), hd * D_VDIM:(hd + 1) * D_VDIM].astype(F32)
            o_ref[0, pl.ds(r0, c), hd * D_VDIM:(hd + 1) * D_VDIM] = (_silu(gate) * yn).astype(o_ref.dtype)
        return tuple(new_states)

    def step(i, states):
        for j in range(RET_INTERLEAVE):
            states = chunk(pl.multiple_of((i * RET_INTERLEAVE + j) * c, c), states)
        return states

    zero = jnp.zeros((LANES, D_VDIM), F32)
    lax.fori_loop(0, seq // (c * RET_INTERLEAVE), step, (zero, zero))


def _retention(rq, rk, rv, rg):
    bsz, seq, rw = rq.shape
    assert rw % LANES == 0 and seq % (RET_CHUNK * RET_INTERLEAVE) == 0 and D_VDIM == LANES
    tabs = _retention_tables()
    n_pairs = rw // LANES
    qk_spec = pl.BlockSpec((1, seq, LANES), lambda b, h: (b, 0, h))
    v_spec = pl.BlockSpec((1, seq, 2 * D_VDIM), lambda b, h: (b, 0, h))
    tab_spec = lambda t: pl.BlockSpec((2,) + t.shape[1:], lambda b, h: (h, 0, 0))
    return pl.pallas_call(
        functools.partial(_retention_kernel, seq=seq),
        grid=(bsz, n_pairs),
        in_specs=[qk_spec, qk_spec, v_spec, v_spec] + [tab_spec(t) for t in tabs],
        out_specs=v_spec,
        out_shape=jax.ShapeDtypeStruct(rv.shape, BF16),
        compiler_params=pltpu.CompilerParams(dimension_semantics=("arbitrary", "arbitrary"),
                                             vmem_limit_bytes=VMEM_LIMIT_BYTES),
        name="retention",
    )(rq, rk, rv, rg, *tabs)


def kernel(x, ev_norm, ev_w_in, ev_a_conv, ev_a_conv_b, ev_a_ln_g, ev_a_ln_b, ev_b_conv, ev_w_out, od_norm, od_w_in, od_w_out, ffn_norm, ffn_w_up, ffn_conv, ffn_conv_b, ffn_w_down, final_norm):
    depth = ffn_norm.shape[0]
    c_width = C_HEADS * HEAD_DIM
    for layer in range(depth):
        j = layer // 2
        mixes = []
        if layer % 2 == 0:
            x = _even_mixer(x, ev_norm[j], ev_w_in[j], ev_a_conv[j], ev_a_conv_b[j], ev_a_ln_g[j], ev_a_ln_b[j],
                            ev_b_conv[j], ev_w_out[j])
        else:
            cq, ck, cv, rq, rk, rv, rg = _odd_proj(x, od_norm[j], od_w_in[j])
            mixes = [(_dilated_attention(cq, ck, cv), od_w_out[j][:c_width]),
                     (_retention(rq, rk, rv, rg), od_w_out[j][c_width:])]
        x = _conv_ffn(x, mixes, ffn_norm[layer], ffn_w_up[layer], ffn_conv[layer], ffn_conv_b[layer],
                      ffn_w_down[layer], final_norm if layer == depth - 1 else None)
    return x
```

```python
import functools

import jax
import jax.numpy as jnp
from jax import lax
from jax.experimental import pallas as pl
from jax.experimental.pallas import tpu as pltpu

F32 = jnp.float32
BF16 = jnp.bfloat16

LANES = 128
SUBLANES = 8
VMEM_LIMIT_BYTES = 56 * 1024 * 1024

HEAD_DIM = 64
A_CONV = 31
SHORT_CONV = 3
DILATIONS = (1, 4, 16)
ATTN_BLOCK = 128
ATTN_INTERLEAVE = 8
RET_CHUNK = 128
RET_INTERLEAVE = 8
C_HEADS = 8
D_HEADS = 4
D_VDIM = 2 * HEAD_DIM
ROPE_THETA = 10000.0
NORM_EPS = 1e-6
NEG_BIG = -1e30
LOG2_E = 1.4426950408889634

TOKEN_TILE = 512
A_HALO = 32
A_ROWS = 64
FF_CHUNK = 256


def _rmsnorm(x, g):
    return x * lax.rsqrt(jnp.mean(x * x, axis=-1, keepdims=True) + NORM_EPS) * g


def _dot(a, b):
    return jnp.dot(a, b, preferred_element_type=F32)


def _dot_nt(a, b):
    return lax.dot_general(a, b, (((1,), (1,)), ((), ())), preferred_element_type=F32)


def _dot_tn(a, b):
    return lax.dot_general(a, b, (((0,), (0,)), ((), ())), preferred_element_type=F32)


def _resident(a):
    return pl.BlockSpec(a.shape, lambda *_: (0,) * a.ndim, pipeline_mode=pl.Buffered(1))


def _lagged_tile_specs(bsz, per_row, block):
    n_tiles = bsz * per_row

    def index(i):
        return i // per_row, i % per_row, 0

    def read(width):
        return pl.BlockSpec((1, block, width), lambda t: index(jnp.minimum(t, n_tiles - 1)))

    def write(width):
        return pl.BlockSpec((1, block, width), lambda t: index(jnp.maximum(t - 1, 0)))

    return n_tiles, read, write


def _silu(x):
    return x * jax.nn.sigmoid(x)


def _conv3(buf, w_ref, n):
    o = SUBLANES
    return (buf[o:o + n, :] * w_ref[2:3, :] + buf[o - 1:o - 1 + n, :] * w_ref[1:2, :]
            + buf[o - 2:o - 2 + n, :] * w_ref[0:1, :])


def _even_kernel(x_ref, g_ref, win_ref, aw_ref, ab_ref, lng_ref, lnb_ref, bw_ref, woa_ref, wob_ref, o_ref,
                 abuf, bbuf, gate, xres, mixa, *, ts, aw, tiles_per_row):
    t = pl.program_id(0)

    @pl.when(t == 0)
    def _():
        for buf in (abuf, bbuf, gate, xres):
            buf[...] = jnp.zeros(buf.shape, buf.dtype)

    x = x_ref[0]
    h = _rmsnorm(x, g_ref[...]).astype(BF16)
    p = _dot(h, win_ref[...])

    n_u = A_HALO // SUBLANES

    def conv_chunk(i):
        r0 = i * A_ROWS
        win = abuf[r0:r0 + A_ROWS + A_HALO, :]
        out = jnp.broadcast_to(ab_ref[...], (A_ROWS, aw))
        for v in range(SUBLANES):
            part = None
            for u in range(n_u):
                s = SUBLANES * u + v
                if s >= A_CONV:
                    continue
                lo = A_HALO - SUBLANES - SUBLANES * u
                term = win[lo:lo + A_ROWS + SUBLANES, :] * aw_ref[A_CONV - 1 - s:A_CONV - s, :]
                part = term if part is None else part + term
            out = out + part[SUBLANES - v:SUBLANES - v + A_ROWS, :]
        mu = jnp.mean(out, axis=-1, keepdims=True)
        oc = out - mu
        y = oc * lax.rsqrt(jnp.mean(oc * oc, axis=-1, keepdims=True) + NORM_EPS)
        y = y * lng_ref[...] + lnb_ref[...]
        mixa[r0:r0 + A_ROWS, :] = _silu(y).astype(BF16)

    for i in range(ts // A_ROWS):
        conv_chunk(i)
    b = gate[...] * _conv3(bbuf, bw_ref, ts)
    o_ref[0] = xres[...] + _dot(mixa[...], woa_ref[...]) + _dot(b.astype(BF16), wob_ref[...])

    keep = t % tiles_per_row != 0
    abuf[0:A_HALO, :] = jnp.where(keep, abuf[ts:ts + A_HALO, :], 0.0)
    bbuf[0:SUBLANES, :] = jnp.where(keep, bbuf[ts:ts + SUBLANES, :], 0.0)
    abuf[A_HALO:A_HALO + ts, :] = p[:, 0:aw] * jax.nn.sigmoid(p[:, aw:2 * aw])
    bbuf[SUBLANES:SUBLANES + ts, :] = p[:, 3 * aw:4 * aw] * p[:, 4 * aw:5 * aw]
    gate[...] = p[:, 2 * aw:3 * aw]
    xres[...] = x


def _even_mixer(x, g, w_in, a_conv, a_conv_b, ln_g, ln_b, b_conv, w_out):
    bsz, seq, d = x.shape
    ts = TOKEN_TILE
    aw = a_conv.shape[1]
    assert seq % ts == 0 and ts % A_ROWS == 0 and A_HALO >= A_CONV - 1 and a_conv.shape[0] == A_CONV
    assert b_conv.shape[0] == SHORT_CONV and w_in.shape[1] == 5 * aw and w_out.shape[0] == 2 * aw
    args = (g[None, :], w_in.astype(BF16), a_conv, a_conv_b[None, :], ln_g[None, :], ln_b[None, :], b_conv,
            w_out[:aw].astype(BF16), w_out[aw:].astype(BF16))
    per_row = seq // ts
    n_tiles, read, write = _lagged_tile_specs(bsz, per_row, ts)
    return pl.pallas_call(
        functools.partial(_even_kernel, ts=ts, aw=aw, tiles_per_row=per_row),
        grid=(n_tiles + 1,),
        in_specs=[read(d)] + [_resident(a) for a in args],
        out_specs=write(d),
        out_shape=jax.ShapeDtypeStruct(x.shape, x.dtype),
        scratch_shapes=[pltpu.VMEM((ts + A_HALO, aw), F32), pltpu.VMEM((ts + SUBLANES, aw), F32),
                        pltpu.VMEM((ts, aw), F32), pltpu.VMEM((ts, d), F32), pltpu.VMEM((ts, aw), BF16)],
        compiler_params=pltpu.CompilerParams(dimension_semantics=("arbitrary",),
                                             vmem_limit_bytes=VMEM_LIMIT_BYTES),
        name="even_mixer",
    )(x, *args)


def _ffn_kernel(*refs, ts, n_chunks, ck, n_mix, final, tiles_per_row, n_tiles):
    x_ref = refs[0]
    mix_refs = refs[1:1 + 2 * n_mix]
    g_ref, wgu_ref, cw_ref, cb_ref, wdn_ref = refs[1 + 2 * n_mix:6 + 2 * n_mix]
    pos = 6 + 2 * n_mix
    fg_ref = refs[pos] if final else None
    pos += int(final)
    o_ref, hbuf, ubuf, utail, hist, acc = refs[pos:pos + 6]
    t = pl.program_id(0)
    last = n_chunks - 1

    @pl.when(t == 0)
    def _():
        acc[...] = jnp.zeros(acc.shape, acc.dtype)
        utail[...] = jnp.zeros(utail.shape, utail.dtype)

    @pl.when(t % tiles_per_row == 0)
    def _():
        hist[...] = jnp.zeros(hist.shape, hist.dtype)

    def up(c, buf):
        buf[0:SUBLANES, :] = hist[c]
        buf[SUBLANES:SUBLANES + ts, :] = _dot(hbuf[...], wgu_ref[c])
        hist[c] = buf[ts:ts + SUBLANES, :]

    def activation(c, buf):
        u = _conv3(buf, cw_ref.at[c], ts) + cb_ref[c]
        return (_silu(u[:, :ck]) * u[:, ck:]).astype(BF16)

    def down(c, buf):
        acc[...] += _dot(activation(c, buf), wdn_ref[c])

    x = x_ref[0]
    for j in range(n_mix):
        x = x + _dot(mix_refs[2 * j][0], mix_refs[2 * j + 1][...])
    hbuf[...] = _rmsnorm(x, g_ref[...]).astype(BF16)
    up(0, ubuf.at[0])

    y = acc[...] + _dot(activation(last, utail), wdn_ref[last])
    if final:
        y = _rmsnorm(y, fg_ref[...])
    o_ref[0] = y
    acc[...] = x

    def chunk_pair(i, carry, tail=False):
        c = 2 * i
        up(c + 1, ubuf.at[1])
        down(c, ubuf.at[0])
        up(c + 2, utail if tail else ubuf.at[0])
        down(c + 1, ubuf.at[1])
        return carry

    @pl.when(t < n_tiles)
    def _():
        lax.fori_loop(0, last // 2 - 1, chunk_pair, 0)
        chunk_pair(last // 2 - 1, 0, tail=True)


def _conv_ffn(x, mixes, g, w_up, conv_w, conv_b, w_down, final_g):
    bsz, seq, d = x.shape
    ts = TOKEN_TILE
    d_ff = w_down.shape[0]
    ck = FF_CHUNK
    assert seq % ts == 0 and d_ff % ck == 0 and conv_w.shape[0] == SHORT_CONV
    nc = d_ff // ck
    assert nc % 2 == 1 and nc >= 3

    def pair(a):
        r = a.shape[0]
        return jnp.concatenate([a[:, :d_ff].reshape(r, nc, ck), a[:, d_ff:].reshape(r, nc, ck)],
                               axis=-1).transpose(1, 0, 2)

    wgu = pair(w_up.astype(BF16))
    cw = jnp.pad(pair(conv_w), ((0, 0), (0, SUBLANES - SHORT_CONV), (0, 0)))
    cb = pair(conv_b[None, :])
    wdn = w_down.astype(BF16).reshape(nc, ck, d)

    per_row = seq // ts
    n_tiles, read, write = _lagged_tile_specs(bsz, per_row, ts)
    args, specs = [x], [read(d)]
    for m, w in mixes:
        args += [m, w.astype(BF16)]
        specs += [read(m.shape[-1]), _resident(w)]
    tail = [g[None, :], wgu, cw, cb, wdn] + ([final_g[None, :]] if final_g is not None else [])
    args += tail
    specs += [_resident(a) for a in tail]
    return pl.pallas_call(
        functools.partial(_ffn_kernel, ts=ts, n_chunks=nc, ck=ck, n_mix=len(mixes), final=final_g is not None,
                          tiles_per_row=per_row, n_tiles=n_tiles),
        grid=(n_tiles + 1,),
        in_specs=specs,
        out_specs=write(d),
        out_shape=jax.ShapeDtypeStruct(x.shape, x.dtype),
        scratch_shapes=[pltpu.VMEM((ts, d), BF16), pltpu.VMEM((2, ts + SUBLANES, 2 * ck), F32),
                        pltpu.VMEM((ts + SUBLANES, 2 * ck), F32), pltpu.VMEM((nc, SUBLANES, 2 * ck), F32),
                        pltpu.VMEM((ts, d), F32)],
        compiler_params=pltpu.CompilerParams(dimension_semantics=("arbitrary",),
                                             vmem_limit_bytes=VMEM_LIMIT_BYTES),
        name="conv_ffn",
    )(*args)


def _rope_tables(seq):
    inv = ROPE_THETA ** (-jnp.arange(0, HEAD_DIM, 2, dtype=F32) / HEAD_DIM)
    ang = jnp.arange(seq, dtype=F32)[:, None] * inv[None, :]
    cos, sin = jnp.cos(ang), jnp.sin(ang)
    reps = LANES // HEAD_DIM
    return jnp.tile(jnp.concatenate([cos, cos], -1), (1, reps)), jnp.tile(jnp.concatenate([-sin, sin], -1), (1, reps))


def _odd_proj_kernel(x_ref, g_ref, win_ref, cos_ref, sin_ref, cq_ref, ck_ref, cv_ref, rq_ref, rk_ref, rv_ref,
                     rg_ref, *, cw, rw, rvw):
    h = _rmsnorm(x_ref[0], g_ref[...]).astype(BF16)
    p = _dot(h, win_ref[...])
    cos, sin = cos_ref[...], sin_ref[...]
    half = HEAD_DIM // 2
    first_half = (lax.broadcasted_iota(jnp.int32, cos.shape, 1) % HEAD_DIM) < half

    def rope_to(dst, col0, width, scale):
        for j in range(width // LANES):
            xb = p[:, col0 + j * LANES:col0 + (j + 1) * LANES]
            partner = jnp.where(first_half, pltpu.roll(xb, LANES - half, 1), pltpu.roll(xb, half, 1))
            y = xb * cos + partner * sin
            if scale != 1.0:
                y = y * scale
            dst[0, :, j * LANES:(j + 1) * LANES] = y.astype(dst.dtype)

    qk_scale = HEAD_DIM ** -0.5
    rope_to(cq_ref, 0, cw, qk_scale)
    rope_to(ck_ref, cw, cw, 1.0)
    cv_ref[0] = p[:, 2 * cw:3 * cw]
    rope_to(rq_ref, 3 * cw, rw, 1.0)
    rope_to(rk_ref, 3 * cw + rw, rw, qk_scale)
    rv_ref[0] = p[:, 3 * cw + 2 * rw:3 * cw + 2 * rw + rvw].astype(rv_ref.dtype)
    rg_ref[0] = p[:, 3 * cw + 2 * rw + rvw:].astype(rg_ref.dtype)


def _odd_proj(x, g, w_in):
    bsz, seq, d = x.shape
    ts = TOKEN_TILE
    cw, rw, rvw = C_HEADS * HEAD_DIM, D_HEADS * HEAD_DIM, D_HEADS * D_VDIM
    assert w_in.shape[1] == 3 * cw + 2 * rw + 2 * rvw and seq % ts == 0
    cos, sin = _rope_tables(seq)
    tile = lambda w: pl.BlockSpec((1, ts, w), lambda b, s: (b, s, 0))
    tab = pl.BlockSpec((ts, LANES), lambda b, s: (s, 0))
    g2, w = g[None, :], w_in.astype(BF16)
    widths = (cw, cw, cw, rw, rw, rvw, rvw)
    dtypes = (F32, F32, F32, BF16, BF16, BF16, BF16)
    return pl.pallas_call(
        functools.partial(_odd_proj_kernel, cw=cw, rw=rw, rvw=rvw),
        grid=(bsz, seq // ts),
        in_specs=[tile(d), _resident(g2), _resident(w), tab, tab],
        out_specs=[tile(wd) for wd in widths],
        out_shape=[jax.ShapeDtypeStruct((bsz, seq, wd), dt) for wd, dt in zip(widths, dtypes)],
        compiler_params=pltpu.CompilerParams(dimension_semantics=("arbitrary", "arbitrary"),
                                             vmem_limit_bytes=VMEM_LIMIT_BYTES),
        name="odd_proj",
    )(x, g2, w, cos, sin)


def _attn_kernel(q_ref, k_ref, v_ref, o_ref, num_ref, m_ref, l_ref, *, seq):
    blk = ATTN_BLOCK
    lane = lax.broadcasted_iota(jnp.int32, (blk, LANES), 1)
    head0 = lane < HEAD_DIM
    qi = lax.broadcasted_iota(jnp.int32, (2 * blk, 2 * blk), 0) & (blk - 1)
    ki = lax.broadcasted_iota(jnp.int32, (2 * blk, 2 * blk), 1)
    valid_both = ((ki < blk) & (ki >= qi)) | ((ki >= blk) & (ki - blk <= qi))
    valid_cur = (lax.broadcasted_iota(jnp.int32, (2 * blk, blk), 1)
                 <= (lax.broadcasted_iota(jnp.int32, (2 * blk, blk), 0) & (blk - 1)))

    def rows(ref, start, n, d):
        if d == 1:
            return ref[0, pl.ds(start, n), :]
        return ref[0, pl.ds(start, n, stride=d), :]

    def put(ref, idx, start, d, val):
        if d == 1:
            ref[idx, pl.ds(start, blk), :] = val
        else:
            ref[idx, pl.ds(start, blk, stride=d), :] = val

    def block(branch, q_start, d, with_prev):
        k_start = q_start - d * blk if with_prev else q_start
        nk = 2 * blk if with_prev else blk
        q = rows(q_ref, q_start, blk, d) * LOG2_E
        k = rows(k_ref, k_start, nk, d).astype(BF16)
        v = rows(v_ref, k_start, nk, d).astype(BF16)
        q2 = jnp.concatenate([jnp.where(head0, q, 0.0), jnp.where(head0, 0.0, q)], axis=0).astype(BF16)
        s = jnp.where(valid_both if with_prev else valid_cur, _dot_nt(q2, k), NEG_BIG)
        m = jnp.max(s, axis=-1, keepdims=True)
        p = jnp.exp2(s - m)
        l = jnp.sum(p, axis=-1, keepdims=True)
        pv = _dot(p.astype(BF16), v)
        put(num_ref, branch, q_start, d, jnp.where(head0, pv[:blk], pv[blk:]))
        put(m_ref, branch, q_start, d, jnp.where(head0, m[:blk], m[blk:]))
        put(l_ref, branch, q_start, d, jnp.where(head0, l[:blk], l[blk:]))

    for branch, d in enumerate(DILATIONS):
        n_blocks = seq // (d * blk)
        step = d * blk
        subs = min(d, ATTN_INTERLEAVE)
        split = ATTN_INTERLEAVE // subs
        span = n_blocks // split

        def issue(r0, n, first, branch=branch, d=d, step=step, subs=subs, split=split, span=span):
            for r in range(subs):
                for j in range(split):
                    start = r0 + r + (n + j * span) * step
                    if d == 1 and not isinstance(start, int):
                        start = pl.multiple_of(start, blk)
                    block(branch, start, d, not (first and j == 0))

        def sub_group(g, carry, issue=issue, subs=subs, span=span):
            r0 = g * subs
            issue(r0, 0, True)

            def body(n, c):
                issue(r0, n, False)
                return c

            lax.fori_loop(1, span, body, 0)
            return carry

        if d == subs:
            sub_group(0, 0)
        else:
            lax.fori_loop(0, d // subs, sub_group, 0)

    def finish(i, carry):
        sl = pl.ds(pl.multiple_of(i * blk, blk), blk)
        ms = [m_ref[b, sl, :] for b in range(len(DILATIONS))]
        m_all = functools.reduce(jnp.maximum, ms)
        num = jnp.zeros((blk, LANES), F32)
        den = jnp.zeros((blk, LANES), F32)
        for b, m in enumerate(ms):
            scale = jnp.exp2(m - m_all)
            num = num + num_ref[b, sl, :] * scale
            den = den + l_ref[b, sl, :] * scale
        o_ref[0, sl, :] = (num / den).astype(o_ref.dtype)
        return carry

    lax.fori_loop(0, seq // blk, finish, 0)


def _dilated_attention(cq, ck, cv):
    bsz, seq, cw = cq.shape
    assert cw % LANES == 0 and LANES == 2 * HEAD_DIM
    for d in DILATIONS:
        assert seq % (d * ATTN_BLOCK) == 0
    spec = pl.BlockSpec((1, seq, LANES), lambda b, h: (b, 0, h))
    return pl.pallas_call(
        functools.partial(_attn_kernel, seq=seq),
        grid=(bsz, cw // LANES),
        in_specs=[spec, spec, spec],
        out_specs=spec,
        out_shape=jax.ShapeDtypeStruct((bsz, seq, cw), BF16),
        scratch_shapes=[pltpu.VMEM((len(DILATIONS), seq, LANES), F32)] * 3,
        compiler_params=pltpu.CompilerParams(dimension_semantics=("arbitrary", "arbitrary"),
                                             vmem_limit_bytes=VMEM_LIMIT_BYTES),
        name="dilated_attention",
    )(cq, ck, cv)


def _retention_tables():
    c = RET_CHUNK
    log_g = jnp.log1p(-(2.0 ** (-5.0 - jnp.arange(D_HEADS, dtype=F32))))
    i = jnp.arange(c, dtype=F32)
    diff = i[:, None] - i[None, :]
    dmat = jnp.where(diff[None] >= 0, jnp.exp(jnp.maximum(diff, 0.0)[None] * log_g[:, None, None]), 0.0)
    kdec = jnp.exp((c - 1 - i)[None, :] * log_g[:, None])
    qdec = jnp.exp((i + 1)[None, :] * log_g[:, None])
    cdec = jnp.exp(c * log_g)
    rep = lambda t: jnp.broadcast_to(t[:, :, None], (D_HEADS, c, LANES))
    return dmat, rep(kdec), rep(qdec), jnp.broadcast_to(cdec[:, None, None], (D_HEADS, SUBLANES, LANES))


def _retention_kernel(q_ref, k_ref, v_ref, g_ref, dmat_ref, kdec_ref, qdec_ref, cdec_ref, o_ref, *, seq):
    c = RET_CHUNK
    lane = lax.broadcasted_iota(jnp.int32, (c, LANES), 1)
    head0 = lane < HEAD_DIM

    def chunk(r0, states):
        q = q_ref[0, pl.ds(r0, c), :].astype(F32)
        k = k_ref[0, pl.ds(r0, c), :].astype(F32)
        new_states = []
        for hd in range(2):
            sel = head0 if hd == 0 else jnp.logical_not(head0)
            qh = jnp.where(sel, q, 0.0)
            kh = jnp.where(sel, k, 0.0)
            v = v_ref[0, pl.ds(r0, c), hd * D_VDIM:(hd + 1) * D_VDIM]
            scores = _dot_nt(qh.astype(BF16), kh.astype(BF16)) * dmat_ref[hd]
            y = _dot(scores.astype(BF16), v)
            y = y + _dot((qh * qdec_ref[hd]).astype(BF16), states[hd].astype(BF16))
            kv = _dot_tn((kh * kdec_ref[hd]).astype(BF16), v)
            new_states.append(states[hd] * cdec_ref[hd, 0:1, :] + kv)
            mu = jnp.mean(y, axis=-1, keepdims=True)
            yc = y - mu
            yn = yc * lax.rsqrt(jnp.mean(yc * yc, axis=-1, keepdims=True) + NORM_EPS)
            gate = g_ref[0, pl.ds(r0, c), hd * D_VDIM:(hd + 1) * D_VDIM].astype(F32)
            o_ref[0, pl.ds(r0, c), hd * D_VDIM:(hd + 1) * D_VDIM] = (_silu(gate) * yn).astype(o_ref.dtype)
        return tuple(new_states)

    def step(i, states):
        for j in range(RET_INTERLEAVE):
            states = chunk(pl.multiple_of((i * RET_INTERLEAVE + j) * c, c), states)
        return states

    zero = jnp.zeros((LANES, D_VDIM), F32)
    lax.fori_loop(0, seq // (c * RET_INTERLEAVE), step, (zero, zero))


def _retention(rq, rk, rv, rg):
    bsz, seq, rw = rq.shape
    assert rw % LANES == 0 and seq % (RET_CHUNK * RET_INTERLEAVE) == 0 and D_VDIM == LANES
    tabs = _retention_tables()
    n_pairs = rw // LANES
    qk_spec = pl.BlockSpec((1, seq, LANES), lambda b, h: (b, 0, h))
    v_spec = pl.BlockSpec((1, seq, 2 * D_VDIM), lambda b, h: (b, 0, h))
    tab_spec = lambda t: pl.BlockSpec((2,) + t.shape[1:], lambda b, h: (h, 0, 0))
    return pl.pallas_call(
        functools.partial(_retention_kernel, seq=seq),
        grid=(bsz, n_pairs),
        in_specs=[qk_spec, qk_spec, v_spec, v_spec] + [tab_spec(t) for t in tabs],
        out_specs=v_spec,
        out_shape=jax.ShapeDtypeStruct(rv.shape, BF16),
        compiler_params=pltpu.CompilerParams(dimension_semantics=("arbitrary", "arbitrary"),
                                             vmem_limit_bytes=VMEM_LIMIT_BYTES),
        name="retention",
    )(rq, rk, rv, rg, *tabs)


def kernel(x, ev_norm, ev_w_in, ev_a_conv, ev_a_conv_b, ev_a_ln_g, ev_a_ln_b, ev_b_conv, ev_w_out, od_norm, od_w_in, od_w_out, ffn_norm, ffn_w_up, ffn_conv, ffn_conv_b, ffn_w_down, final_norm):
    depth = ffn_norm.shape[0]
    c_width = C_HEADS * HEAD_DIM
    for layer in range(depth):
        j = layer // 2
        mixes = []
        if layer % 2 == 0:
            x = _even_mixer(x, ev_norm[j], ev_w_in[j], ev_a_conv[j], ev_a_conv_b[j], ev_a_ln_g[j], ev_a_ln_b[j],
                            ev_b_conv[j], ev_w_out[j])
        else:
            cq, ck, cv, rq, rk, rv, rg = _odd_proj(x, od_norm[j], od_w_in[j])
            mixes = [(_dilated_attention(cq, ck, cv), od_w_out[j][:c_width]),
                     (_retention(rq, rk, rv, rg), od_w_out[j][c_width:])]
        x = _conv_ffn(x, mixes, ffn_norm[layer], ffn_w_up[layer], ffn_conv[layer], ffn_conv_b[layer],
                      ffn_w_down[layer], final_norm if layer == depth - 1 else None)
    return x
```

```python
import functools

import jax
import jax.numpy as jnp
from jax import lax
from jax.experimental import pallas as pl
from jax.experimental.pallas import tpu as pltpu

F32 = jnp.float32
BF16 = jnp.bfloat16

LANES = 128
SUBLANES = 8
VMEM_LIMIT_BYTES = 56 * 1024 * 1024

HEAD_DIM = 64
A_CONV = 31
SHORT_CONV = 3
DILATIONS = (1, 4, 16)
ATTN_BLOCK = 128
ATTN_INTERLEAVE = (16, 16, 16)
RET_CHUNK = 128
RET_INTERLEAVE = 8
C_HEADS = 8
D_HEADS = 4
D_VDIM = 2 * HEAD_DIM
ROPE_THETA = 10000.0
NORM_EPS = 1e-6
NEG_BIG = -1e30
LOG2_E = 1.4426950408889634

TOKEN_TILE = 512
A_HALO = 32
A_ROWS = 128
FF_CHUNK = 256


def _rmsnorm(x, g):
    return x * lax.rsqrt(jnp.mean(x * x, axis=-1, keepdims=True) + NORM_EPS) * g


def _dot(a, b):
    return jnp.dot(a, b, preferred_element_type=F32)


def _dot_nt(a, b):
    return lax.dot_general(a, b, (((1,), (1,)), ((), ())), preferred_element_type=F32)


def _dot_tn(a, b):
    return lax.dot_general(a, b, (((0,), (0,)), ((), ())), preferred_element_type=F32)


def _resident(a):
    return pl.BlockSpec(a.shape, lambda *_: (0,) * a.ndim, pipeline_mode=pl.Buffered(1))


def _lagged_tile_specs(bsz, per_row, block):
    n_tiles = bsz * per_row

    def index(i):
        return i // per_row, i % per_row, 0

    def read(width):
        return pl.BlockSpec((1, block, width), lambda t: index(jnp.minimum(t, n_tiles - 1)))

    def write(width):
        return pl.BlockSpec((1, block, width), lambda t: index(jnp.maximum(t - 1, 0)))

    return n_tiles, read, write


def _silu(x):
    return x * jax.nn.sigmoid(x)


def _conv3(buf, w_ref, n):
    o = SUBLANES
    return (buf[o:o + n, :] * w_ref[2:3, :] + buf[o - 1:o - 1 + n, :] * w_ref[1:2, :]
            + buf[o - 2:o - 2 + n, :] * w_ref[0:1, :])


def _even_kernel(x_ref, g_ref, win_ref, aw_ref, ab_ref, lng_ref, lnb_ref, bw_ref, woa_ref, wob_ref, o_ref,
                 abuf, bbuf, gate, xres, mixa, *, ts, aw, tiles_per_row):
    t = pl.program_id(0)

    @pl.when(t == 0)
    def _():
        for buf in (abuf, bbuf, gate, xres):
            buf[...] = jnp.zeros(buf.shape, buf.dtype)

    x = x_ref[0]
    h = _rmsnorm(x, g_ref[...]).astype(BF16)
    p = _dot(h, win_ref[...])

    n_u = A_HALO // SUBLANES

    def conv_chunk(i):
        r0 = i * A_ROWS
        cols = []
        for c0 in range(0, aw, LANES):
            win = abuf[r0:r0 + A_ROWS + A_HALO, c0:c0 + LANES]
            out = jnp.broadcast_to(ab_ref[:, c0:c0 + LANES], (A_ROWS, LANES))
            for v in range(SUBLANES):
                part = None
                for u in range(n_u):
                    s = SUBLANES * u + v
                    if s >= A_CONV:
                        continue
                    lo = A_HALO - SUBLANES - SUBLANES * u
                    term = win[lo:lo + A_ROWS + SUBLANES, :] * aw_ref[A_CONV - 1 - s:A_CONV - s, c0:c0 + LANES]
                    part = term if part is None else part + term
                out = out + part[SUBLANES - v:SUBLANES - v + A_ROWS, :]
            cols.append(out)
        out = jnp.concatenate(cols, axis=1)
        mu = jnp.mean(out, axis=-1, keepdims=True)
        oc = out - mu
        y = oc * lax.rsqrt(jnp.mean(oc * oc, axis=-1, keepdims=True) + NORM_EPS)
        y = y * lng_ref[...] + lnb_ref[...]
        mixa[r0:r0 + A_ROWS, :] = _silu(y).astype(BF16)

    for i in range(ts // A_ROWS):
        conv_chunk(i)
    b = gate[...] * _conv3(bbuf, bw_ref, ts)
    o_ref[0] = xres[...] + _dot(mixa[...], woa_ref[...]) + _dot(b.astype(BF16), wob_ref[...])

    keep = t % tiles_per_row != 0
    abuf[0:A_HALO, :] = jnp.where(keep, abuf[ts:ts + A_HALO, :], 0.0)
    bbuf[0:SUBLANES, :] = jnp.where(keep, bbuf[ts:ts + SUBLANES, :], 0.0)
    abuf[A_HALO:A_HALO + ts, :] = p[:, 0:aw] * jax.nn.sigmoid(p[:, aw:2 * aw])
    bbuf[SUBLANES:SUBLANES + ts, :] = p[:, 3 * aw:4 * aw] * p[:, 4 * aw:5 * aw]
    gate[...] = p[:, 2 * aw:3 * aw]
    xres[...] = x


def _even_mixer(x, g, w_in, a_conv, a_conv_b, ln_g, ln_b, b_conv, w_out):
    bsz, seq, d = x.shape
    ts = TOKEN_TILE
    aw = a_conv.shape[1]
    assert seq % ts == 0 and ts % A_ROWS == 0 and A_HALO >= A_CONV - 1 and a_conv.shape[0] == A_CONV
    assert b_conv.shape[0] == SHORT_CONV and w_in.shape[1] == 5 * aw and w_out.shape[0] == 2 * aw
    args = (g[None, :], w_in.astype(BF16), a_conv, a_conv_b[None, :], ln_g[None, :], ln_b[None, :], b_conv,
            w_out[:aw].astype(BF16), w_out[aw:].astype(BF16))
    per_row = seq // ts
    n_tiles, read, write = _lagged_tile_specs(bsz, per_row, ts)
    return pl.pallas_call(
        functools.partial(_even_kernel, ts=ts, aw=aw, tiles_per_row=per_row),
        grid=(n_tiles + 1,),
        in_specs=[read(d)] + [_resident(a) for a in args],
        out_specs=write(d),
        out_shape=jax.ShapeDtypeStruct(x.shape, x.dtype),
        scratch_shapes=[pltpu.VMEM((ts + A_HALO, aw), F32), pltpu.VMEM((ts + SUBLANES, aw), F32),
                        pltpu.VMEM((ts, aw), F32), pltpu.VMEM((ts, d), F32), pltpu.VMEM((ts, aw), BF16)],
        compiler_params=pltpu.CompilerParams(dimension_semantics=("arbitrary",),
                                             vmem_limit_bytes=VMEM_LIMIT_BYTES),
        name="even_mixer",
    )(x, *args)


def _ffn_kernel(*refs, ts, n_chunks, ck, n_mix, final, tiles_per_row, n_tiles):
    x_ref = refs[0]
    mix_refs = refs[1:1 + 2 * n_mix]
    g_ref, wgu_ref, cw_ref, cb_ref, wdn_ref = refs[1 + 2 * n_mix:6 + 2 * n_mix]
    pos = 6 + 2 * n_mix
    fg_ref = refs[pos] if final else None
    pos += int(final)
    o_ref, hbuf, ubuf, utail, hist, acc = refs[pos:pos + 6]
    t = pl.program_id(0)
    last = n_chunks - 1

    @pl.when(t == 0)
    def _():
        acc[...] = jnp.zeros(acc.shape, acc.dtype)
        utail[...] = jnp.zeros(utail.shape, utail.dtype)

    @pl.when(t % tiles_per_row == 0)
    def _():
        hist[...] = jnp.zeros(hist.shape, hist.dtype)

    def up(c, buf):
        buf[0:SUBLANES, :] = hist[c]
        d_ff = n_chunks * ck
        for half in range(2):
            cols = pl.ds(pl.multiple_of(half * d_ff + c * ck, LANES), ck)
            buf[SUBLANES:SUBLANES + ts, half * ck:(half + 1) * ck] = _dot(hbuf[...], wgu_ref[:, cols])
        hist[c] = buf[ts:ts + SUBLANES, :]

    def activation(c, buf):
        u = _conv3(buf, cw_ref.at[c], ts) + cb_ref[c]
        return (_silu(u[:, :ck]) * u[:, ck:]).astype(BF16)

    def down(c, buf):
        acc[...] += _dot(activation(c, buf), wdn_ref[c])

    x = x_ref[0]
    for j in range(n_mix):
        x = x + _dot(mix_refs[2 * j][0], mix_refs[2 * j + 1][...])
    hbuf[...] = _rmsnorm(x, g_ref[...]).astype(BF16)
    up(0, ubuf.at[0])

    y = acc[...] + _dot(activation(last, utail), wdn_ref[last])
    if final:
        y = _rmsnorm(y, fg_ref[...])
    o_ref[0] = y
    acc[...] = x

    def chunk_pair(i, carry, tail=False):
        c = 2 * i
        up(c + 1, ubuf.at[1])
        down(c, ubuf.at[0])
        up(c + 2, utail if tail else ubuf.at[0])
        down(c + 1, ubuf.at[1])
        return carry

    @pl.when(t < n_tiles)
    def _():
        lax.fori_loop(0, last // 2 - 1, chunk_pair, 0)
        chunk_pair(last // 2 - 1, 0, tail=True)


def _conv_ffn(x, mixes, g, w_up, conv_w, conv_b, w_down, final_g):
    bsz, seq, d = x.shape
    ts = TOKEN_TILE
    d_ff = w_down.shape[0]
    ck = FF_CHUNK
    assert seq % ts == 0 and d_ff % ck == 0 and conv_w.shape[0] == SHORT_CONV
    nc = d_ff // ck
    assert nc % 2 == 1 and nc >= 3

    def pair(a):
        r = a.shape[0]
        return jnp.concatenate([a[:, :d_ff].reshape(r, nc, ck), a[:, d_ff:].reshape(r, nc, ck)],
                               axis=-1).transpose(1, 0, 2)

    wgu = w_up.astype(BF16)
    cw = jnp.pad(pair(conv_w), ((0, 0), (0, SUBLANES - SHORT_CONV), (0, 0)))
    cb = pair(conv_b[None, :])
    wdn = w_down.astype(BF16).reshape(nc, ck, d)

    per_row = seq // ts
    n_tiles, read, write = _lagged_tile_specs(bsz, per_row, ts)
    args, specs = [x], [read(d)]
    for m, w in mixes:
        args += [m, w.astype(BF16)]
        specs += [read(m.shape[-1]), _resident(w)]
    tail = [g[None, :], wgu, cw, cb, wdn] + ([final_g[None, :]] if final_g is not None else [])
    args += tail
    specs += [_resident(a) for a in tail]
    return pl.pallas_call(
        functools.partial(_ffn_kernel, ts=ts, n_chunks=nc, ck=ck, n_mix=len(mixes), final=final_g is not None,
                          tiles_per_row=per_row, n_tiles=n_tiles),
        grid=(n_tiles + 1,),
        in_specs=specs,
        out_specs=write(d),
        out_shape=jax.ShapeDtypeStruct(x.shape, x.dtype),
        scratch_shapes=[pltpu.VMEM((ts, d), BF16), pltpu.VMEM((2, ts + SUBLANES, 2 * ck), F32),
                        pltpu.VMEM((ts + SUBLANES, 2 * ck), F32), pltpu.VMEM((nc, SUBLANES, 2 * ck), F32),
                        pltpu.VMEM((ts, d), F32)],
        compiler_params=pltpu.CompilerParams(dimension_semantics=("arbitrary",),
                                             vmem_limit_bytes=VMEM_LIMIT_BYTES),
        name="conv_ffn",
    )(*args)


def _rope_tables(seq):
    inv = ROPE_THETA ** (-jnp.arange(0, HEAD_DIM, 2, dtype=F32) / HEAD_DIM)
    ang = jnp.arange(seq, dtype=F32)[:, None] * inv[None, :]
    cos, sin = jnp.cos(ang), jnp.sin(ang)
    reps = LANES // HEAD_DIM
    return jnp.tile(jnp.concatenate([cos, cos], -1), (1, reps)), jnp.tile(jnp.concatenate([-sin, sin], -1), (1, reps))


def _odd_proj_kernel(x_ref, g_ref, win_ref, cos_ref, sin_ref, cq_ref, ck_ref, cv_ref, rq_ref, rk_ref, rv_ref,
                     rg_ref, *, cw, rw, rvw):
    h = _rmsnorm(x_ref[0], g_ref[...]).astype(BF16)
    p = _dot(h, win_ref[...])
    cos, sin = cos_ref[...], sin_ref[...]
    half = HEAD_DIM // 2
    first_half = (lax.broadcasted_iota(jnp.int32, cos.shape, 1) % HEAD_DIM) < half

    def rope_to(dst, col0, width, scale):
        for j in range(width // LANES):
            xb = p[:, col0 + j * LANES:col0 + (j + 1) * LANES]
            partner = jnp.where(first_half, pltpu.roll(xb, LANES - half, 1), pltpu.roll(xb, half, 1))
            y = xb * cos + partner * sin
            if scale != 1.0:
                y = y * scale
            dst[0, :, j * LANES:(j + 1) * LANES] = y.astype(dst.dtype)

    qk_scale = HEAD_DIM ** -0.5
    rope_to(cq_ref, 0, cw, qk_scale)
    rope_to(ck_ref, cw, cw, 1.0)
    cv_ref[0] = p[:, 2 * cw:3 * cw]
    rope_to(rq_ref, 3 * cw, rw, 1.0)
    rope_to(rk_ref, 3 * cw + rw, rw, qk_scale)
    rv_ref[0] = p[:, 3 * cw + 2 * rw:3 * cw + 2 * rw + rvw].astype(rv_ref.dtype)
    rg_ref[0] = p[:, 3 * cw + 2 * rw + rvw:].astype(rg_ref.dtype)


def _odd_proj(x, g, w_in):
    bsz, seq, d = x.shape
    ts = TOKEN_TILE
    cw, rw, rvw = C_HEADS * HEAD_DIM, D_HEADS * HEAD_DIM, D_HEADS * D_VDIM
    assert w_in.shape[1] == 3 * cw + 2 * rw + 2 * rvw and seq % ts == 0
    cos, sin = _rope_tables(seq)
    tile = lambda w: pl.BlockSpec((1, ts, w), lambda b, s: (b, s, 0))
    tab = pl.BlockSpec((ts, LANES), lambda b, s: (s, 0))
    g2, w = g[None, :], w_in.astype(BF16)
    widths = (cw, cw, cw, rw, rw, rvw, rvw)
    dtypes = (F32, F32, F32, BF16, BF16, BF16, BF16)
    return pl.pallas_call(
        functools.partial(_odd_proj_kernel, cw=cw, rw=rw, rvw=rvw),
        grid=(bsz, seq // ts),
        in_specs=[tile(d), _resident(g2), _resident(w), tab, tab],
        out_specs=[tile(wd) for wd in widths],
        out_shape=[jax.ShapeDtypeStruct((bsz, seq, wd), dt) for wd, dt in zip(widths, dtypes)],
        compiler_params=pltpu.CompilerParams(dimension_semantics=("arbitrary", "arbitrary"),
                                             vmem_limit_bytes=VMEM_LIMIT_BYTES),
        name="odd_proj",
    )(x, g2, w, cos, sin)


def _attn_kernel(q_ref, k_ref, v_ref, o_ref, num_ref, m_ref, l_ref, *, seq):
    blk = ATTN_BLOCK
    lane = lax.broadcasted_iota(jnp.int32, (blk, LANES), 1)
    head0 = lane < HEAD_DIM
    qi = lax.broadcasted_iota(jnp.int32, (2 * blk, 2 * blk), 0) & (blk - 1)
    ki = lax.broadcasted_iota(jnp.int32, (2 * blk, 2 * blk), 1)
    valid_both = ((ki < blk) & (ki >= qi)) | ((ki >= blk) & (ki - blk <= qi))
    valid_cur = (lax.broadcasted_iota(jnp.int32, (2 * blk, blk), 1)
                 <= (lax.broadcasted_iota(jnp.int32, (2 * blk, blk), 0) & (blk - 1)))

    def rows(ref, start, n, d):
        if d == 1:
            return ref[0, pl.ds(start, n), :]
        return ref[0, pl.ds(start, n, stride=d), :]

    def put(ref, idx, start, d, val):
        if d == 1:
            ref[idx, pl.ds(start, blk), :] = val
        else:
            ref[idx, pl.ds(start, blk, stride=d), :] = val

    def block(branch, q_start, d, with_prev):
        k_start = q_start - d * blk if with_prev else q_start
        nk = 2 * blk if with_prev else blk
        q = rows(q_ref, q_start, blk, d) * LOG2_E
        k = rows(k_ref, k_start, nk, d).astype(BF16)
        v = rows(v_ref, k_start, nk, d).astype(BF16)
        q2 = jnp.concatenate([jnp.where(head0, q, 0.0), jnp.where(head0, 0.0, q)], axis=0).astype(BF16)
        s = jnp.where(valid_both if with_prev else valid_cur, _dot_nt(q2, k), NEG_BIG)
        m = jnp.max(s, axis=-1, keepdims=True)
        p = jnp.exp2(s - m)
        l = jnp.sum(p, axis=-1, keepdims=True)
        pv = _dot(p.astype(BF16), v)
        put(num_ref, branch, q_start, d, jnp.where(head0, pv[:blk], pv[blk:]))
        put(m_ref, branch, q_start, d, jnp.where(head0, m[:blk], m[blk:]))
        put(l_ref, branch, q_start, d, jnp.where(head0, l[:blk], l[blk:]))

    for branch, d in enumerate(DILATIONS):
        n_blocks = seq // (d * blk)
        step = d * blk
        subs = min(d, ATTN_INTERLEAVE[branch])
        split = ATTN_INTERLEAVE[branch] // subs
        span = n_blocks // split

        def issue(r0, n, first, branch=branch, d=d, step=step, subs=subs, split=split, span=span):
            for r in range(subs):
                for j in range(split):
                    start = r0 + r + (n + j * span) * step
                    if d == 1 and not isinstance(start, int):
                        start = pl.multiple_of(start, blk)
                    block(branch, start, d, not (first and j == 0))

        def sub_group(g, carry, issue=issue, subs=subs, span=span):
            r0 = g * subs
            issue(r0, 0, True)

            def body(n, c):
                issue(r0, n, False)
                return c

            lax.fori_loop(1, span, body, 0)
            return carry

        if d == subs:
            sub_group(0, 0)
        else:
            lax.fori_loop(0, d // subs, sub_group, 0)

    def finish(i, carry):
        sl = pl.ds(pl.multiple_of(i * blk, blk), blk)
        ms = [m_ref[b, sl, :] for b in range(len(DILATIONS))]
        m_all = functools.reduce(jnp.maximum, ms)
        num = jnp.zeros((blk, LANES), F32)
        den = jnp.zeros((blk, LANES), F32)
        for b, m in enumerate(ms):
            scale = jnp.exp2(m - m_all)
            num = num + num_ref[b, sl, :] * scale
            den = den + l_ref[b, sl, :] * scale
        o_ref[0, sl, :] = (num / den).astype(o_ref.dtype)
        return carry

    lax.fori_loop(0, seq // blk, finish, 0)


def _dilated_attention(cq, ck, cv):
    bsz, seq, cw = cq.shape
    assert cw % LANES == 0 and LANES == 2 * HEAD_DIM
    for d in DILATIONS:
        assert seq % (d * ATTN_BLOCK) == 0
    spec = pl.BlockSpec((1, seq, LANES), lambda b, h: (b, 0, h))
    return pl.pallas_call(
        functools.partial(_attn_kernel, seq=seq),
        grid=(bsz, cw // LANES),
        in_specs=[spec, spec, spec],
        out_specs=spec,
        out_shape=jax.ShapeDtypeStruct((bsz, seq, cw), BF16),
        scratch_shapes=[pltpu.VMEM((len(DILATIONS), seq, LANES), F32)] * 3,
        compiler_params=pltpu.CompilerParams(dimension_semantics=("arbitrary", "arbitrary"),
                                             vmem_limit_bytes=VMEM_LIMIT_BYTES),
        name="dilated_attention",
    )(cq, ck, cv)


def _retention_tables():
    c = RET_CHUNK
    log_g = jnp.log1p(-(2.0 ** (-5.0 - jnp.arange(D_HEADS, dtype=F32))))
    i = jnp.arange(c, dtype=F32)
    diff = i[:, None] - i[None, :]
    dmat = jnp.where(diff[None] >= 0, jnp.exp(jnp.maximum(diff, 0.0)[None] * log_g[:, None, None]), 0.0)
    kdec = jnp.exp((c - 1 - i)[None, :] * log_g[:, None])
    qdec = jnp.exp((i + 1)[None, :] * log_g[:, None])
    cdec = jnp.exp(c * log_g)
    rep = lambda t: jnp.broadcast_to(t[:, :, None], (D_HEADS, c, LANES))
    return dmat, rep(kdec), rep(qdec), jnp.broadcast_to(cdec[:, None, None], (D_HEADS, SUBLANES, LANES))


def _retention_kernel(q_ref, k_ref, v_ref, g_ref, dmat_ref, kdec_ref, qdec_ref, cdec_ref, o_ref, *, seq):
    c = RET_CHUNK
    lane = lax.broadcasted_iota(jnp.int32, (c, LANES), 1)
    head0 = lane < HEAD_DIM

    def chunk(r0, states):
        q = q_ref[0, pl.ds(r0, c), :].astype(F32)
        k = k_ref[0, pl.ds(r0, c), :].astype(F32)
        new_states = []
        for hd in range(2):
            sel = head0 if hd == 0 else jnp.logical_not(head0)
            qh = jnp.where(sel, q, 0.0)
            kh = jnp.where(sel, k, 0.0)
            v = v_ref[0, pl.ds(r0, c), hd * D_VDIM:(hd + 1) * D_VDIM]
            scores = _dot_nt(qh.astype(BF16), kh.astype(BF16)) * dmat_ref[hd]
            y = _dot(scores.astype(BF16), v)
            y = y + _dot((qh * qdec_ref[hd]).astype(BF16), states[hd].astype(BF16))
            kv = _dot_tn((kh * kdec_ref[hd]).astype(BF16), v)
            new_states.append(states[hd] * cdec_ref[hd, 0:1, :] + kv)
            mu = jnp.mean(y, axis=-1, keepdims=True)
            yc = y - mu
            yn = yc * lax.rsqrt(jnp.mean(yc * yc, axis=-1, keepdims=True) + NORM_EPS)
            gate = g_ref[0, pl.ds(r0, c), hd * D_VDIM:(hd + 1) * D_VDIM].astype(F32)
            o_ref[0, pl.ds(r0, c), hd * D_VDIM:(hd + 1) * D_VDIM] = (_silu(gate) * yn).astype(o_ref.dtype)
        return tuple(new_states)

    def step(i, states):
        for j in range(RET_INTERLEAVE):
            states = chunk(pl.multiple_of((i * RET_INTERLEAVE + j) * c, c), states)
        return states

    zero = jnp.zeros((LANES, D_VDIM), F32)
    lax.fori_loop(0, seq // (c * RET_INTERLEAVE), step, (zero, zero))


def _retention(rq, rk, rv, rg):
    bsz, seq, rw = rq.shape
    assert rw % LANES == 0 and seq % (RET_CHUNK * RET_INTERLEAVE) == 0 and D_VDIM == LANES
    tabs = _retention_tables()
    n_pairs = rw // LANES
    qk_spec = pl.BlockSpec((1, seq, LANES), lambda b, h: (b, 0, h))
    v_spec = pl.BlockSpec((1, seq, 2 * D_VDIM), lambda b, h: (b, 0, h))
    tab_spec = lambda t: pl.BlockSpec((2,) + t.shape[1:], lambda b, h: (h, 0, 0))
    return pl.pallas_call(
        functools.partial(_retention_kernel, seq=seq),
        grid=(bsz, n_pairs),
        in_specs=[qk_spec, qk_spec, v_spec, v_spec] + [tab_spec(t) for t in tabs],
        out_specs=v_spec,
        out_shape=jax.ShapeDtypeStruct(rv.shape, BF16),
        compiler_params=pltpu.CompilerParams(dimension_semantics=("arbitrary", "arbitrary"),
                                             vmem_limit_bytes=VMEM_LIMIT_BYTES),
        name="retention",
    )(rq, rk, rv, rg, *tabs)


def kernel(x, ev_norm, ev_w_in, ev_a_conv, ev_a_conv_b, ev_a_ln_g, ev_a_ln_b, ev_b_conv, ev_w_out, od_norm, od_w_in, od_w_out, ffn_norm, ffn_w_up, ffn_conv, ffn_conv_b, ffn_w_down, final_norm):
    depth = ffn_norm.shape[0]
    c_width = C_HEADS * HEAD_DIM
    for layer in range(depth):
        j = layer // 2
        mixes = []
        if layer % 2 == 0:
            x = _even_mixer(x, ev_norm[j], ev_w_in[j], ev_a_conv[j], ev_a_conv_b[j], ev_a_ln_g[j], ev_a_ln_b[j],
                            ev_b_conv[j], ev_w_out[j])
        else:
            cq, ck, cv, rq, rk, rv, rg = _odd_proj(x, od_norm[j], od_w_in[j])
            mixes = [(_dilated_attention(cq, ck, cv), od_w_out[j][:c_width]),
                     (_retention(rq, rk, rv, rg), od_w_out[j][c_width:])]
        x = _conv_ffn(x, mixes, ffn_norm[layer], ffn_w_up[layer], ffn_conv[layer], ffn_conv_b[layer],
                      ffn_w_down[layer], final_norm if layer == depth - 1 else None)
    return x
```

```python
import functools

import jax
import jax.numpy as jnp
from jax import lax
from jax.experimental import pallas as pl
from jax.experimental.pallas import tpu as pltpu

F32 = jnp.float32
BF16 = jnp.bfloat16

LANES = 128
SUBLANES = 8
VMEM_LIMIT_BYTES = 56 * 1024 * 1024

HEAD_DIM = 64
A_CONV = 31
SHORT_CONV = 3
DILATIONS = (1, 4, 16)
ATTN_BLOCK = 128
ATTN_INTERLEAVE = (16, 16, 16)
RET_CHUNK = 128
RET_INTERLEAVE = 8
C_HEADS = 8
D_HEADS = 4
D_VDIM = 2 * HEAD_DIM
ROPE_THETA = 10000.0
NORM_EPS = 1e-6
NEG_BIG = -1e30
LOG2_E = 1.4426950408889634

TOKEN_TILE = 512
A_HALO = 32
A_ROWS = 128
FF_CHUNK = 256


def _rmsnorm(x, g):
    return x * lax.rsqrt(jnp.mean(x * x, axis=-1, keepdims=True) + NORM_EPS) * g


def _dot(a, b):
    return jnp.dot(a, b, preferred_element_type=F32)


def _dot_nt(a, b):
    return lax.dot_general(a, b, (((1,), (1,)), ((), ())), preferred_element_type=F32)


def _dot_tn(a, b):
    return lax.dot_general(a, b, (((0,), (0,)), ((), ())), preferred_element_type=F32)


def _resident(a):
    return pl.BlockSpec(a.shape, lambda *_: (0,) * a.ndim, pipeline_mode=pl.Buffered(1))


def _lagged_tile_specs(bsz, per_row, block):
    n_tiles = bsz * per_row

    def index(i):
        return i // per_row, i % per_row, 0

    def read(width):
        return pl.BlockSpec((1, block, width), lambda t: index(jnp.minimum(t, n_tiles - 1)))

    def write(width):
        return pl.BlockSpec((1, block, width), lambda t: index(jnp.maximum(t - 1, 0)))

    return n_tiles, read, write


def _silu(x):
    return x * jax.nn.sigmoid(x)


def _conv3(buf, w_ref, n):
    o = SUBLANES
    return (buf[o:o + n, :] * w_ref[2:3, :] + buf[o - 1:o - 1 + n, :] * w_ref[1:2, :]
            + buf[o - 2:o - 2 + n, :] * w_ref[0:1, :])


def _even_kernel(x_ref, g_ref, win_ref, aw_ref, ab_ref, lng_ref, lnb_ref, bw_ref, woa_ref, wob_ref, o_ref,
                 abuf, bbuf, gate, xres, mixa, *, ts, aw, tiles_per_row):
    t = pl.program_id(0)

    @pl.when(t == 0)
    def _():
        for buf in (abuf, bbuf, gate, xres):
            buf[...] = jnp.zeros(buf.shape, buf.dtype)

    x = x_ref[0]
    h = _rmsnorm(x, g_ref[...]).astype(BF16)
    p = _dot(h, win_ref[...])

    n_u = A_HALO // SUBLANES

    def conv_chunk(i):
        r0 = i * A_ROWS
        cols = []
        for c0 in range(0, aw, LANES):
            win = abuf[r0:r0 + A_ROWS + A_HALO, c0:c0 + LANES]
            out = jnp.broadcast_to(ab_ref[:, c0:c0 + LANES], (A_ROWS, LANES))
            for v in range(SUBLANES):
                part = None
                for u in range(n_u):
                    s = SUBLANES * u + v
                    if s >= A_CONV:
                        continue
                    lo = A_HALO - SUBLANES - SUBLANES * u
                    term = win[lo:lo + A_ROWS + SUBLANES, :] * aw_ref[A_CONV - 1 - s:A_CONV - s, c0:c0 + LANES]
                    part = term if part is None else part + term
                out = out + part[SUBLANES - v:SUBLANES - v + A_ROWS, :]
            cols.append(out)
        out = jnp.concatenate(cols, axis=1)
        mu = jnp.mean(out, axis=-1, keepdims=True)
        oc = out - mu
        y = oc * lax.rsqrt(jnp.mean(oc * oc, axis=-1, keepdims=True) + NORM_EPS)
        y = y * lng_ref[...] + lnb_ref[...]
        mixa[r0:r0 + A_ROWS, :] = _silu(y).astype(BF16)

    for i in range(ts // A_ROWS):
        conv_chunk(i)
    b = gate[...] * _conv3(bbuf, bw_ref, ts)
    o_ref[0] = xres[...] + _dot(mixa[...], woa_ref[...]) + _dot(b.astype(BF16), wob_ref[...])

    keep = t % tiles_per_row != 0
    abuf[0:A_HALO, :] = jnp.where(keep, abuf[ts:ts + A_HALO, :], 0.0)
    bbuf[0:SUBLANES, :] = jnp.where(keep, bbuf[ts:ts + SUBLANES, :], 0.0)
    abuf[A_HALO:A_HALO + ts, :] = p[:, 0:aw] * jax.nn.sigmoid(p[:, aw:2 * aw])
    bbuf[SUBLANES:SUBLANES + ts, :] = p[:, 3 * aw:4 * aw] * p[:, 4 * aw:5 * aw]
    gate[...] = p[:, 2 * aw:3 * aw]
    xres[...] = x


def _even_mixer(x, g, w_in, a_conv, a_conv_b, ln_g, ln_b, b_conv, w_out):
    bsz, seq, d = x.shape
    ts = TOKEN_TILE
    aw = a_conv.shape[1]
    assert seq % ts == 0 and ts % A_ROWS == 0 and A_HALO >= A_CONV - 1 and a_conv.shape[0] == A_CONV
    assert b_conv.shape[0] == SHORT_CONV and w_in.shape[1] == 5 * aw and w_out.shape[0] == 2 * aw
    args = (g[None, :], w_in.astype(BF16), a_conv, a_conv_b[None, :], ln_g[None, :], ln_b[None, :], b_conv,
            w_out[:aw].astype(BF16), w_out[aw:].astype(BF16))
    per_row = seq // ts
    n_tiles, read, write = _lagged_tile_specs(bsz, per_row, ts)
    return pl.pallas_call(
        functools.partial(_even_kernel, ts=ts, aw=aw, tiles_per_row=per_row),
        grid=(n_tiles + 1,),
        in_specs=[read(d)] + [_resident(a) for a in args],
        out_specs=write(d),
        out_shape=jax.ShapeDtypeStruct(x.shape, x.dtype),
        scratch_shapes=[pltpu.VMEM((ts + A_HALO, aw), F32), pltpu.VMEM((ts + SUBLANES, aw), F32),
                        pltpu.VMEM((ts, aw), F32), pltpu.VMEM((ts, d), F32), pltpu.VMEM((ts, aw), BF16)],
        compiler_params=pltpu.CompilerParams(dimension_semantics=("arbitrary",),
                                             vmem_limit_bytes=VMEM_LIMIT_BYTES),
        name="even_mixer",
    )(x, *args)


def _ffn_kernel(*refs, ts, n_chunks, ck, n_mix, final, tiles_per_row, n_tiles):
    x_ref = refs[0]
    mix_refs = refs[1:1 + 2 * n_mix]
    g_ref, wgu_ref, cw_ref, cb_ref, wdn_ref = refs[1 + 2 * n_mix:6 + 2 * n_mix]
    pos = 6 + 2 * n_mix
    fg_ref = refs[pos] if final else None
    pos += int(final)
    o_ref, hbuf, ubuf, utail, hist, acc, xres, ybuf = refs[pos:pos + 8]
    t = pl.program_id(0)
    last = n_chunks - 1
    d_ff = n_chunks * ck
    half_rows = ts // 2
    ck_slabs = ck // LANES
    d = acc.shape[1]

    @pl.when(t == 0)
    def _():
        for buf in (acc, utail, xres):
            buf[...] = jnp.zeros(buf.shape, buf.dtype)

    @pl.when(t % tiles_per_row == 0)
    def _():
        hist[...] = jnp.zeros(hist.shape, hist.dtype)

    def up(c, buf):
        buf[:, 0:SUBLANES, :] = hist[c]
        for half in range(2):
            cols = pl.ds(pl.multiple_of(half * d_ff + c * ck, LANES), ck)
            r = _dot(hbuf[...], wgu_ref[:, cols])
            for s in range(ck_slabs):
                buf[half * ck_slabs + s, SUBLANES:SUBLANES + ts, :] = r[:, s * LANES:(s + 1) * LANES]
        hist[c] = buf[:, ts:ts + SUBLANES, :]

    def conv(c, buf, slab):
        lanes = slice(slab * LANES, (slab + 1) * LANES)
        w0, w1, w2 = (cw_ref[c, k:k + 1, lanes] for k in range(SHORT_CONV))
        b = cb_ref[c, :, lanes]
        rows = [buf[slab, pl.ds(SUBLANES - 2 + k, half_rows, stride=2), :] for k in range(4)]
        even = rows[2] * w2 + rows[1] * w1 + rows[0] * w0 + b
        odd = rows[3] * w2 + rows[2] * w1 + rows[1] * w0 + b
        return jnp.concatenate([even, odd], axis=0)

    def activation(c, buf):
        cols = [(_silu(conv(c, buf, s)) * conv(c, buf, ck_slabs + s)).astype(BF16) for s in range(ck_slabs)]
        return jnp.concatenate(cols, axis=1)

    def down(c, buf):
        acc[...] += _dot(activation(c, buf), wdn_ref[c])

    x = x_ref[0]
    for j in range(n_mix):
        x = x + _dot(mix_refs[2 * j][0], mix_refs[2 * j + 1][...])
    hbuf[...] = _rmsnorm(x, g_ref[...]).astype(BF16)
    up(0, ubuf.at[0])

    y = acc[...] + _dot(activation(last, utail), wdn_ref[last])
    for s in range(d // LANES):
        ybuf[s, pl.ds(0, half_rows, stride=2), :] = y[:half_rows, s * LANES:(s + 1) * LANES]
        ybuf[s, pl.ds(1, half_rows, stride=2), :] = y[half_rows:, s * LANES:(s + 1) * LANES]
    y = xres[...] + jnp.concatenate([ybuf[s] for s in range(d // LANES)], axis=1)
    if final:
        y = _rmsnorm(y, fg_ref[...])
    o_ref[0] = y
    acc[...] = jnp.zeros(acc.shape, acc.dtype)
    xres[...] = x

    def chunk_pair(i, carry, tail=False):
        c = 2 * i
        up(c + 1, ubuf.at[1])
        down(c, ubuf.at[0])
        up(c + 2, utail if tail else ubuf.at[0])
        down(c + 1, ubuf.at[1])
        return carry

    @pl.when(t < n_tiles)
    def _():
        lax.fori_loop(0, last // 2 - 1, chunk_pair, 0)
        chunk_pair(last // 2 - 1, 0, tail=True)


def _conv_ffn(x, mixes, g, w_up, conv_w, conv_b, w_down, final_g):
    bsz, seq, d = x.shape
    ts = TOKEN_TILE
    d_ff = w_down.shape[0]
    ck = FF_CHUNK
    assert seq % ts == 0 and d_ff % ck == 0 and conv_w.shape[0] == SHORT_CONV
    nc = d_ff // ck
    assert nc % 2 == 1 and nc >= 3
    assert ck % LANES == 0 and d % LANES == 0 and ts % (2 * SUBLANES) == 0
    slabs = 2 * ck // LANES

    def pair(a):
        r = a.shape[0]
        return jnp.concatenate([a[:, :d_ff].reshape(r, nc, ck), a[:, d_ff:].reshape(r, nc, ck)],
                               axis=-1).transpose(1, 0, 2)

    wgu = w_up.astype(BF16)
    cw = jnp.pad(pair(conv_w), ((0, 0), (0, SUBLANES - SHORT_CONV), (0, 0)))
    cb = pair(conv_b[None, :])
    wdn = w_down.astype(BF16).reshape(nc, ck, d)

    per_row = seq // ts
    n_tiles, read, write = _lagged_tile_specs(bsz, per_row, ts)
    args, specs = [x], [read(d)]
    for m, w in mixes:
        args += [m, w.astype(BF16)]
        specs += [read(m.shape[-1]), _resident(w)]
    tail = [g[None, :], wgu, cw, cb, wdn] + ([final_g[None, :]] if final_g is not None else [])
    args += tail
    specs += [_resident(a) for a in tail]
    return pl.pallas_call(
        functools.partial(_ffn_kernel, ts=ts, n_chunks=nc, ck=ck, n_mix=len(mixes), final=final_g is not None,
                          tiles_per_row=per_row, n_tiles=n_tiles),
        grid=(n_tiles + 1,),
        in_specs=specs,
        out_specs=write(d),
        out_shape=jax.ShapeDtypeStruct(x.shape, x.dtype),
        scratch_shapes=[pltpu.VMEM((ts, d), BF16), pltpu.VMEM((2, slabs, ts + SUBLANES, LANES), F32),
                        pltpu.VMEM((slabs, ts + SUBLANES, LANES), F32),
                        pltpu.VMEM((nc, slabs, SUBLANES, LANES), F32), pltpu.VMEM((ts, d), F32),
                        pltpu.VMEM((ts, d), F32), pltpu.VMEM((d // LANES, ts, LANES), F32)],
        compiler_params=pltpu.CompilerParams(dimension_semantics=("arbitrary",),
                                             vmem_limit_bytes=VMEM_LIMIT_BYTES),
        name="conv_ffn",
    )(*args)


def _rope_tables(seq):
    inv = ROPE_THETA ** (-jnp.arange(0, HEAD_DIM, 2, dtype=F32) / HEAD_DIM)
    ang = jnp.arange(seq, dtype=F32)[:, None] * inv[None, :]
    cos, sin = jnp.cos(ang), jnp.sin(ang)
    reps = LANES // HEAD_DIM
    return jnp.tile(jnp.concatenate([cos, cos], -1), (1, reps)), jnp.tile(jnp.concatenate([-sin, sin], -1), (1, reps))


def _odd_proj_kernel(x_ref, g_ref, win_ref, cos_ref, sin_ref, cq_ref, ck_ref, cv_ref, rq_ref, rk_ref, rv_ref,
                     rg_ref, *, cw, rw, rvw):
    h = _rmsnorm(x_ref[0], g_ref[...]).astype(BF16)
    p = _dot(h, win_ref[...])
    cos, sin = cos_ref[...], sin_ref[...]
    half = HEAD_DIM // 2
    first_half = (lax.broadcasted_iota(jnp.int32, cos.shape, 1) % HEAD_DIM) < half

    def rope_to(dst, col0, width, scale):
        for j in range(width // LANES):
            xb = p[:, col0 + j * LANES:col0 + (j + 1) * LANES]
            partner = jnp.where(first_half, pltpu.roll(xb, LANES - half, 1), pltpu.roll(xb, half, 1))
            y = xb * cos + partner * sin
            if scale != 1.0:
                y = y * scale
            dst[0, :, j * LANES:(j + 1) * LANES] = y.astype(dst.dtype)

    qk_scale = HEAD_DIM ** -0.5
    rope_to(cq_ref, 0, cw, qk_scale)
    rope_to(ck_ref, cw, cw, 1.0)
    cv_ref[0] = p[:, 2 * cw:3 * cw]
    rope_to(rq_ref, 3 * cw, rw, 1.0)
    rope_to(rk_ref, 3 * cw + rw, rw, qk_scale)
    rv_ref[0] = p[:, 3 * cw + 2 * rw:3 * cw + 2 * rw + rvw].astype(rv_ref.dtype)
    rg_ref[0] = p[:, 3 * cw + 2 * rw + rvw:].astype(rg_ref.dtype)


def _odd_proj(x, g, w_in):
    bsz, seq, d = x.shape
    ts = TOKEN_TILE
    cw, rw, rvw = C_HEADS * HEAD_DIM, D_HEADS * HEAD_DIM, D_HEADS * D_VDIM
    assert w_in.shape[1] == 3 * cw + 2 * rw + 2 * rvw and seq % ts == 0
    cos, sin = _rope_tables(seq)
    tile = lambda w: pl.BlockSpec((1, ts, w), lambda b, s: (b, s, 0))
    tab = pl.BlockSpec((ts, LANES), lambda b, s: (s, 0))
    g2, w = g[None, :], w_in.astype(BF16)
    widths = (cw, cw, cw, rw, rw, rvw, rvw)
    dtypes = (F32, F32, F32, BF16, BF16, BF16, BF16)
    return pl.pallas_call(
        functools.partial(_odd_proj_kernel, cw=cw, rw=rw, rvw=rvw),
        grid=(bsz, seq // ts),
        in_specs=[tile(d), _resident(g2), _resident(w), tab, tab],
        out_specs=[tile(wd) for wd in widths],
        out_shape=[jax.ShapeDtypeStruct((bsz, seq, wd), dt) for wd, dt in zip(widths, dtypes)],
        compiler_params=pltpu.CompilerParams(dimension_semantics=("arbitrary", "arbitrary"),
                                             vmem_limit_bytes=VMEM_LIMIT_BYTES),
        name="odd_proj",
    )(x, g2, w, cos, sin)


def _attn_kernel(q_ref, k_ref, v_ref, o_ref, num_ref, m_ref, l_ref, *, seq):
    blk = ATTN_BLOCK
    lane = lax.broadcasted_iota(jnp.int32, (blk, LANES), 1)
    head0 = lane < HEAD_DIM
    qi = lax.broadcasted_iota(jnp.int32, (2 * blk, 2 * blk), 0) & (blk - 1)
    ki = lax.broadcasted_iota(jnp.int32, (2 * blk, 2 * blk), 1)
    valid_both = ((ki < blk) & (ki >= qi)) | ((ki >= blk) & (ki - blk <= qi))
    valid_cur = (lax.broadcasted_iota(jnp.int32, (2 * blk, blk), 1)
                 <= (lax.broadcasted_iota(jnp.int32, (2 * blk, blk), 0) & (blk - 1)))

    def rows(ref, start, n, d):
        if d == 1:
            return ref[0, pl.ds(start, n), :]
        return ref[0, pl.ds(start, n, stride=d), :]

    def put(ref, idx, start, d, val):
        if d == 1:
            ref[idx, pl.ds(start, blk), :] = val
        else:
            ref[idx, pl.ds(start, blk, stride=d), :] = val

    def block(branch, q_start, d, with_prev):
        k_start = q_start - d * blk if with_prev else q_start
        nk = 2 * blk if with_prev else blk
        q = rows(q_ref, q_start, blk, d) * LOG2_E
        k = rows(k_ref, k_start, nk, d).astype(BF16)
        v = rows(v_ref, k_start, nk, d).astype(BF16)
        q2 = jnp.concatenate([jnp.where(head0, q, 0.0), jnp.where(head0, 0.0, q)], axis=0).astype(BF16)
        s = jnp.where(valid_both if with_prev else valid_cur, _dot_nt(q2, k), NEG_BIG)
        m = jnp.max(s, axis=-1, keepdims=True)
        p = jnp.exp2(s - m)
        l = jnp.sum(p, axis=-1, keepdims=True)
        pv = _dot(p.astype(BF16), v)
        put(num_ref, branch, q_start, d, jnp.where(head0, pv[:blk], pv[blk:]))
        put(m_ref, branch, q_start, d, jnp.where(head0, m[:blk], m[blk:]))
        put(l_ref, branch, q_start, d, jnp.where(head0, l[:blk], l[blk:]))

    for branch, d in enumerate(DILATIONS):
        n_blocks = seq // (d * blk)
        step = d * blk
        subs = min(d, ATTN_INTERLEAVE[branch])
        split = ATTN_INTERLEAVE[branch] // subs
        span = n_blocks // split

        def issue(r0, n, first, branch=branch, d=d, step=step, subs=subs, split=split, span=span):
            for r in range(subs):
                for j in range(split):
                    start = r0 + r + (n + j * span) * step
                    if d == 1 and not isinstance(start, int):
                        start = pl.multiple_of(start, blk)
                    block(branch, start, d, not (first and j == 0))

        def sub_group(g, carry, issue=issue, subs=subs, span=span):
            r0 = g * subs
            issue(r0, 0, True)

            def body(n, c):
                issue(r0, n, False)
                return c

            lax.fori_loop(1, span, body, 0)
            return carry

        if d == subs:
            sub_group(0, 0)
        else:
            lax.fori_loop(0, d // subs, sub_group, 0)

    def finish(i, carry):
        sl = pl.ds(pl.multiple_of(i * blk, blk), blk)
        ms = [m_ref[b, sl, :] for b in range(len(DILATIONS))]
        m_all = functools.reduce(jnp.maximum, ms)
        num = jnp.zeros((blk, LANES), F32)
        den = jnp.zeros((blk, LANES), F32)
        for b, m in enumerate(ms):
            scale = jnp.exp2(m - m_all)
            num = num + num_ref[b, sl, :] * scale
            den = den + l_ref[b, sl, :] * scale
        o_ref[0, sl, :] = (num / den).astype(o_ref.dtype)
        return carry

    lax.fori_loop(0, seq // blk, finish, 0)


def _dilated_attention(cq, ck, cv):
    bsz, seq, cw = cq.shape
    assert cw % LANES == 0 and LANES == 2 * HEAD_DIM
    for d in DILATIONS:
        assert seq % (d * ATTN_BLOCK) == 0
    spec = pl.BlockSpec((1, seq, LANES), lambda b, h: (b, 0, h))
    return pl.pallas_call(
        functools.partial(_attn_kernel, seq=seq),
        grid=(bsz, cw // LANES),
        in_specs=[spec, spec, spec],
        out_specs=spec,
        out_shape=jax.ShapeDtypeStruct((bsz, seq, cw), BF16),
        scratch_shapes=[pltpu.VMEM((len(DILATIONS), seq, LANES), F32)] * 3,
        compiler_params=pltpu.CompilerParams(dimension_semantics=("arbitrary", "arbitrary"),
                                             vmem_limit_bytes=VMEM_LIMIT_BYTES),
        name="dilated_attention",
    )(cq, ck, cv)


def _retention_tables():
    c = RET_CHUNK
    log_g = jnp.log1p(-(2.0 ** (-5.0 - jnp.arange(D_HEADS, dtype=F32))))
    i = jnp.arange(c, dtype=F32)
    diff = i[:, None] - i[None, :]
    dmat = jnp.where(diff[None] >= 0, jnp.exp(jnp.maximum(diff, 0.0)[None] * log_g[:, None, None]), 0.0)
    kdec = jnp.exp((c - 1 - i)[None, :] * log_g[:, None])
    qdec = jnp.exp((i + 1)[None, :] * log_g[:, None])
    cdec = jnp.exp(c * log_g)
    rep = lambda t: jnp.broadcast_to(t[:, :, None], (D_HEADS, c, LANES))
    return dmat, rep(kdec), rep(qdec), jnp.broadcast_to(cdec[:, None, None], (D_HEADS, SUBLANES, LANES))


def _retention_kernel(q_ref, k_ref, v_ref, g_ref, dmat_ref, kdec_ref, qdec_ref, cdec_ref, o_ref, *, seq):
    c = RET_CHUNK
    lane = lax.broadcasted_iota(jnp.int32, (c, LANES), 1)
    head0 = lane < HEAD_DIM

    def chunk(r0, states):
        q = q_ref[0, pl.ds(r0, c), :].astype(F32)
        k = k_ref[0, pl.ds(r0, c), :].astype(F32)
        new_states = []
        for hd in range(2):
            sel = head0 if hd == 0 else jnp.logical_not(head0)
            qh = jnp.where(sel, q, 0.0)
            kh = jnp.where(sel, k, 0.0)
            v = v_ref[0, pl.ds(r0, c), hd * D_VDIM:(hd + 1) * D_VDIM]
            scores = _dot_nt(qh.astype(BF16), kh.astype(BF16)) * dmat_ref[hd]
            y = _dot(scores.astype(BF16), v)
            y = y + _dot((qh * qdec_ref[hd]).astype(BF16), states[hd].astype(BF16))
            kv = _dot_tn((kh * kdec_ref[hd]).astype(BF16), v)
            new_states.append(states[hd] * cdec_ref[hd, 0:1, :] + kv)
            mu = jnp.mean(y, axis=-1, keepdims=True)
            yc = y - mu
            yn = yc * lax.rsqrt(jnp.mean(yc * yc, axis=-1, keepdims=True) + NORM_EPS)
            gate = g_ref[0, pl.ds(r0, c), hd * D_VDIM:(hd + 1) * D_VDIM].astype(F32)
            o_ref[0, pl.ds(r0, c), hd * D_VDIM:(hd + 1) * D_VDIM] = (_silu(gate) * yn).astype(o_ref.dtype)
        return tuple(new_states)

    def step(i, states):
        for j in range(RET_INTERLEAVE):
            states = chunk(pl.multiple_of((i * RET_INTERLEAVE + j) * c, c), states)
        return states

    zero = jnp.zeros((LANES, D_VDIM), F32)
    lax.fori_loop(0, seq // (c * RET_INTERLEAVE), step, (zero, zero))


def _retention(rq, rk, rv, rg):
    bsz, seq, rw = rq.shape
    assert rw % LANES == 0 and seq % (RET_CHUNK * RET_INTERLEAVE) == 0 and D_VDIM == LANES
    tabs = _retention_tables()
    n_pairs = rw // LANES
    qk_spec = pl.BlockSpec((1, seq, LANES), lambda b, h: (b, 0, h))
    v_spec = pl.BlockSpec((1, seq, 2 * D_VDIM), lambda b, h: (b, 0, h))
    tab_spec = lambda t: pl.BlockSpec((2,) + t.shape[1:], lambda b, h: (h, 0, 0))
    return pl.pallas_call(
        functools.partial(_retention_kernel, seq=seq),
        grid=(bsz, n_pairs),
        in_specs=[qk_spec, qk_spec, v_spec, v_spec] + [tab_spec(t) for t in tabs],
        out_specs=v_spec,
        out_shape=jax.ShapeDtypeStruct(rv.shape, BF16),
        compiler_params=pltpu.CompilerParams(dimension_semantics=("arbitrary", "arbitrary"),
                                             vmem_limit_bytes=VMEM_LIMIT_BYTES),
        name="retention",
    )(rq, rk, rv, rg, *tabs)


def kernel(x, ev_norm, ev_w_in, ev_a_conv, ev_a_conv_b, ev_a_ln_g, ev_a_ln_b, ev_b_conv, ev_w_out, od_norm, od_w_in, od_w_out, ffn_norm, ffn_w_up, ffn_conv, ffn_conv_b, ffn_w_down, final_norm):
    depth = ffn_norm.shape[0]
    c_width = C_HEADS * HEAD_DIM
    for layer in range(depth):
        j = layer // 2
        mixes = []
        if layer % 2 == 0:
            x = _even_mixer(x, ev_norm[j], ev_w_in[j], ev_a_conv[j], ev_a_conv_b[j], ev_a_ln_g[j], ev_a_ln_b[j],
                            ev_b_conv[j], ev_w_out[j])
        else:
            cq, ck, cv, rq, rk, rv, rg = _odd_proj(x, od_norm[j], od_w_in[j])
            mixes = [(_dilated_attention(cq, ck, cv), od_w_out[j][:c_width]),
                     (_retention(rq, rk, rv, rg), od_w_out[j][c_width:])]
        x = _conv_ffn(x, mixes, ffn_norm[layer], ffn_w_up[layer], ffn_conv[layer], ffn_conv_b[layer],
                      ffn_w_down[layer], final_norm if layer == depth - 1 else None)
    return x
```

```python
import functools

import jax
import jax.numpy as jnp
from jax import lax
from jax.experimental import pallas as pl
from jax.experimental.pallas import tpu as pltpu

F32 = jnp.float32
BF16 = jnp.bfloat16

LANES = 128
SUBLANES = 8
VMEM_LIMIT_BYTES = 56 * 1024 * 1024

HEAD_DIM = 64
A_CONV = 31
SHORT_CONV = 3
DILATIONS = (1, 4, 16)
ATTN_BLOCK = 128
ATTN_INTERLEAVE = (16, 16, 16)
RET_CHUNK = 128
RET_INTERLEAVE = 8
C_HEADS = 8
D_HEADS = 4
D_VDIM = 2 * HEAD_DIM
ROPE_THETA = 10000.0
NORM_EPS = 1e-6
NEG_BIG = -1e30
LOG2_E = 1.4426950408889634

TOKEN_TILE = 512
A_HALO = 32
A_ROWS = 128
FF_CHUNK = 256


def _rmsnorm(x, g):
    return x * lax.rsqrt(jnp.mean(x * x, axis=-1, keepdims=True) + NORM_EPS) * g


def _dot(a, b):
    return jnp.dot(a, b, preferred_element_type=F32)


def _dot_nt(a, b):
    return lax.dot_general(a, b, (((1,), (1,)), ((), ())), preferred_element_type=F32)


def _dot_tn(a, b):
    return lax.dot_general(a, b, (((0,), (0,)), ((), ())), preferred_element_type=F32)


def _resident(a):
    return pl.BlockSpec(a.shape, lambda *_: (0,) * a.ndim, pipeline_mode=pl.Buffered(1))


def _lagged_tile_specs(bsz, per_row, block):
    n_tiles = bsz * per_row

    def index(i):
        return i // per_row, i % per_row, 0

    def read(width):
        return pl.BlockSpec((1, block, width), lambda t: index(jnp.minimum(t, n_tiles - 1)))

    def write(width):
        return pl.BlockSpec((1, block, width), lambda t: index(jnp.maximum(t - 1, 0)))

    return n_tiles, read, write


def _silu(x):
    return x * jax.nn.sigmoid(x)


def _conv3(buf, w_ref, n):
    o = SUBLANES
    return (buf[o:o + n, :] * w_ref[2:3, :] + buf[o - 1:o - 1 + n, :] * w_ref[1:2, :]
            + buf[o - 2:o - 2 + n, :] * w_ref[0:1, :])


def _even_kernel(x_ref, g_ref, win_ref, aw_ref, ab_ref, lng_ref, lnb_ref, bw_ref, woa_ref, wob_ref, o_ref,
                 abuf, bbuf, gate, xres, mixa, *, ts, aw, tiles_per_row):
    t = pl.program_id(0)

    @pl.when(t == 0)
    def _():
        for buf in (abuf, bbuf, gate, xres):
            buf[...] = jnp.zeros(buf.shape, buf.dtype)

    x = x_ref[0]
    h = _rmsnorm(x, g_ref[...]).astype(BF16)
    p = _dot(h, win_ref[...])

    n_u = A_HALO // SUBLANES

    def conv_chunk(i):
        r0 = i * A_ROWS
        cols = []
        for c0 in range(0, aw, LANES):
            win = abuf[r0:r0 + A_ROWS + A_HALO, c0:c0 + LANES]
            out = jnp.broadcast_to(ab_ref[:, c0:c0 + LANES], (A_ROWS, LANES))
            for v in range(SUBLANES):
                part = None
                for u in range(n_u):
                    s = SUBLANES * u + v
                    if s >= A_CONV:
                        continue
                    lo = A_HALO - SUBLANES - SUBLANES * u
                    term = win[lo:lo + A_ROWS + SUBLANES, :] * aw_ref[A_CONV - 1 - s:A_CONV - s, c0:c0 + LANES]
                    part = term if part is None else part + term
                out = out + part[SUBLANES - v:SUBLANES - v + A_ROWS, :]
            cols.append(out)
        out = jnp.concatenate(cols, axis=1)
        mu = jnp.mean(out, axis=-1, keepdims=True)
        oc = out - mu
        y = oc * lax.rsqrt(jnp.mean(oc * oc, axis=-1, keepdims=True) + NORM_EPS)
        y = y * lng_ref[...] + lnb_ref[...]
        mixa[r0:r0 + A_ROWS, :] = _silu(y).astype(BF16)

    for i in range(ts // A_ROWS):
        conv_chunk(i)
    b = gate[...] * _conv3(bbuf, bw_ref, ts)
    o_ref[0] = xres[...] + _dot(mixa[...], woa_ref[...]) + _dot(b.astype(BF16), wob_ref[...])

    keep = t % tiles_per_row != 0
    abuf[0:A_HALO, :] = jnp.where(keep, abuf[ts:ts + A_HALO, :], 0.0)
    bbuf[0:SUBLANES, :] = jnp.where(keep, bbuf[ts:ts + SUBLANES, :], 0.0)
    abuf[A_HALO:A_HALO + ts, :] = p[:, 0:aw] * jax.nn.sigmoid(p[:, aw:2 * aw])
    bbuf[SUBLANES:SUBLANES + ts, :] = p[:, 3 * aw:4 * aw] * p[:, 4 * aw:5 * aw]
    gate[...] = p[:, 2 * aw:3 * aw]
    xres[...] = x


def _even_mixer(x, g, w_in, a_conv, a_conv_b, ln_g, ln_b, b_conv, w_out):
    bsz, seq, d = x.shape
    ts = TOKEN_TILE
    aw = a_conv.shape[1]
    assert seq % ts == 0 and ts % A_ROWS == 0 and A_HALO >= A_CONV - 1 and a_conv.shape[0] == A_CONV
    assert b_conv.shape[0] == SHORT_CONV and w_in.shape[1] == 5 * aw and w_out.shape[0] == 2 * aw
    args = (g[None, :], w_in.astype(BF16), a_conv, a_conv_b[None, :], ln_g[None, :], ln_b[None, :], b_conv,
            w_out[:aw].astype(BF16), w_out[aw:].astype(BF16))
    per_row = seq // ts
    n_tiles, read, write = _lagged_tile_specs(bsz, per_row, ts)
    return pl.pallas_call(
        functools.partial(_even_kernel, ts=ts, aw=aw, tiles_per_row=per_row),
        grid=(n_tiles + 1,),
        in_specs=[read(d)] + [_resident(a) for a in args],
        out_specs=write(d),
        out_shape=jax.ShapeDtypeStruct(x.shape, x.dtype),
        scratch_shapes=[pltpu.VMEM((ts + A_HALO, aw), F32), pltpu.VMEM((ts + SUBLANES, aw), F32),
                        pltpu.VMEM((ts, aw), F32), pltpu.VMEM((ts, d), F32), pltpu.VMEM((ts, aw), BF16)],
        compiler_params=pltpu.CompilerParams(dimension_semantics=("arbitrary",),
                                             vmem_limit_bytes=VMEM_LIMIT_BYTES),
        name="even_mixer",
    )(x, *args)


def _ffn_kernel(*refs, ts, n_chunks, ck, n_mix, final, tiles_per_row, n_tiles):
    x_ref = refs[0]
    mix_refs = refs[1:1 + 2 * n_mix]
    g_ref, wgu_ref, cw_ref, cb_ref, wdn_ref = refs[1 + 2 * n_mix:6 + 2 * n_mix]
    pos = 6 + 2 * n_mix
    fg_ref = refs[pos] if final else None
    pos += int(final)
    o_ref, hbuf, ubuf, utail, hist, acc, xres, ybuf = refs[pos:pos + 8]
    t = pl.program_id(0)
    last = n_chunks - 1
    d_ff = n_chunks * ck
    half_rows = ts // 2
    ck_slabs = ck // LANES
    d = acc.shape[1]

    @pl.when(t == 0)
    def _():
        for buf in (acc, utail, xres):
            buf[...] = jnp.zeros(buf.shape, buf.dtype)

    @pl.when(t % tiles_per_row == 0)
    def _():
        hist[...] = jnp.zeros(hist.shape, hist.dtype)

    def up(c, buf):
        buf[:, 0:SUBLANES, :] = hist[c]
        for half in range(2):
            cols = pl.ds(pl.multiple_of(half * d_ff + c * ck, LANES), ck)
            r = _dot(hbuf[...], wgu_ref[:, cols])
            for s in range(ck_slabs):
                buf[half * ck_slabs + s, SUBLANES:SUBLANES + ts, :] = r[:, s * LANES:(s + 1) * LANES]
        hist[c] = buf[:, ts:ts + SUBLANES, :]

    def conv(c, buf, slab):
        lanes = slice(slab * LANES, (slab + 1) * LANES)
        w0, w1, w2 = (cw_ref[c, k:k + 1, lanes] for k in range(SHORT_CONV))
        b = cb_ref[c, :, lanes]
        rows = [buf[slab, pl.ds(SUBLANES - 2 + k, half_rows, stride=2), :] for k in range(4)]
        even = rows[2] * w2 + rows[1] * w1 + rows[0] * w0 + b
        odd = rows[3] * w2 + rows[2] * w1 + rows[1] * w0 + b
        return jnp.concatenate([even, odd], axis=0)

    def activation(c, buf):
        cols = [(_silu(conv(c, buf, s)) * conv(c, buf, ck_slabs + s)).astype(BF16) for s in range(ck_slabs)]
        return jnp.concatenate(cols, axis=1)

    def down(c, buf):
        acc[...] += _dot(activation(c, buf), wdn_ref[c])

    x = x_ref[0]
    for j in range(n_mix):
        x = x + _dot(mix_refs[2 * j][0], mix_refs[2 * j + 1][...])
    hbuf[...] = _rmsnorm(x, g_ref[...]).astype(BF16)
    up(0, ubuf.at[0])

    y = acc[...] + _dot(activation(last, utail), wdn_ref[last])
    for s in range(d // LANES):
        ybuf[s, pl.ds(0, half_rows, stride=2), :] = y[:half_rows, s * LANES:(s + 1) * LANES]
        ybuf[s, pl.ds(1, half_rows, stride=2), :] = y[half_rows:, s * LANES:(s + 1) * LANES]
    y = xres[...] + jnp.concatenate([ybuf[s] for s in range(d // LANES)], axis=1)
    if final:
        y = _rmsnorm(y, fg_ref[...])
    o_ref[0] = y
    acc[...] = jnp.zeros(acc.shape, acc.dtype)
    xres[...] = x

    def chunk_pair(i, carry, tail=False):
        c = 2 * i
        up(c + 1, ubuf.at[1])
        down(c, ubuf.at[0])
        up(c + 2, utail if tail else ubuf.at[0])
        down(c + 1, ubuf.at[1])
        return carry

    @pl.when(t < n_tiles)
    def _():
        lax.fori_loop(0, last // 2 - 1, chunk_pair, 0)
        chunk_pair(last // 2 - 1, 0, tail=True)


def _conv_ffn(x, mixes, g, w_up, conv_w, conv_b, w_down, final_g):
    bsz, seq, d = x.shape
    ts = TOKEN_TILE
    d_ff = w_down.shape[0]
    ck = FF_CHUNK
    assert seq % ts == 0 and d_ff % ck == 0 and conv_w.shape[0] == SHORT_CONV
    nc = d_ff // ck
    assert nc % 2 == 1 and nc >= 3
    assert ck % LANES == 0 and d % LANES == 0 and ts % (2 * SUBLANES) == 0
    slabs = 2 * ck // LANES

    def pair(a):
        r = a.shape[0]
        return jnp.concatenate([a[:, :d_ff].reshape(r, nc, ck), a[:, d_ff:].reshape(r, nc, ck)],
                               axis=-1).transpose(1, 0, 2)

    wgu = w_up.astype(BF16)
    cw = jnp.pad(pair(conv_w), ((0, 0), (0, SUBLANES - SHORT_CONV), (0, 0)))
    cb = pair(conv_b[None, :])
    wdn = w_down.astype(BF16).reshape(nc, ck, d)

    per_row = seq // ts
    n_tiles, read, write = _lagged_tile_specs(bsz, per_row, ts)
    args, specs = [x], [read(d)]
    for m, w in mixes:
        args += [m, w.astype(BF16)]
        specs += [read(m.shape[-1]), _resident(w)]
    tail = [g[None, :], wgu, cw, cb, wdn] + ([final_g[None, :]] if final_g is not None else [])
    args += tail
    specs += [_resident(a) for a in tail]
    return pl.pallas_call(
        functools.partial(_ffn_kernel, ts=ts, n_chunks=nc, ck=ck, n_mix=len(mixes), final=final_g is not None,
                          tiles_per_row=per_row, n_tiles=n_tiles),
        grid=(n_tiles + 1,),
        in_specs=specs,
        out_specs=write(d),
        out_shape=jax.ShapeDtypeStruct(x.shape, x.dtype),
        scratch_shapes=[pltpu.VMEM((ts, d), BF16), pltpu.VMEM((2, slabs, ts + SUBLANES, LANES), F32),
                        pltpu.VMEM((slabs, ts + SUBLANES, LANES), F32),
                        pltpu.VMEM((nc, slabs, SUBLANES, LANES), F32), pltpu.VMEM((ts, d), F32),
                        pltpu.VMEM((ts, d), F32), pltpu.VMEM((d // LANES, ts, LANES), F32)],
        compiler_params=pltpu.CompilerParams(dimension_semantics=("arbitrary",),
                                             vmem_limit_bytes=VMEM_LIMIT_BYTES),
        name="conv_ffn",
    )(*args)


def _rope_tables(seq):
    inv = ROPE_THETA ** (-jnp.arange(0, HEAD_DIM, 2, dtype=F32) / HEAD_DIM)
    ang = jnp.arange(seq, dtype=F32)[:, None] * inv[None, :]
    cos, sin = jnp.cos(ang), jnp.sin(ang)
    reps = LANES // HEAD_DIM
    return jnp.tile(jnp.concatenate([cos, cos], -1), (1, reps)), jnp.tile(jnp.concatenate([-sin, sin], -1), (1, reps))


def _odd_proj_kernel(x_ref, g_ref, win_ref, cos_ref, sin_ref, cq_ref, ck_ref, cv_ref, rq_ref, rk_ref, rv_ref,
                     rg_ref, *, cw, rw, rvw):
    h = _rmsnorm(x_ref[0], g_ref[...]).astype(BF16)
    p = _dot(h, win_ref[...])
    cos, sin = cos_ref[...], sin_ref[...]
    half = HEAD_DIM // 2
    first_half = (lax.broadcasted_iota(jnp.int32, cos.shape, 1) % HEAD_DIM) < half

    def rope_to(dst, col0, width, scale):
        for j in range(width // LANES):
            xb = p[:, col0 + j * LANES:col0 + (j + 1) * LANES]
            partner = jnp.where(first_half, pltpu.roll(xb, LANES - half, 1), pltpu.roll(xb, half, 1))
            y = xb * cos + partner * sin
            if scale != 1.0:
                y = y * scale
            dst[0, :, j * LANES:(j + 1) * LANES] = y.astype(dst.dtype)

    qk_scale = HEAD_DIM ** -0.5
    rope_to(cq_ref, 0, cw, qk_scale)
    rope_to(ck_ref, cw, cw, 1.0)
    cv_ref[0] = p[:, 2 * cw:3 * cw]
    rope_to(rq_ref, 3 * cw, rw, 1.0)
    rope_to(rk_ref, 3 * cw + rw, rw, qk_scale)
    rv_ref[0] = p[:, 3 * cw + 2 * rw:3 * cw + 2 * rw + rvw].astype(rv_ref.dtype)
    rg_ref[0] = p[:, 3 * cw + 2 * rw + rvw:].astype(rg_ref.dtype)


def _odd_proj(x, g, w_in):
    bsz, seq, d = x.shape
    ts = TOKEN_TILE
    cw, rw, rvw = C_HEADS * HEAD_DIM, D_HEADS * HEAD_DIM, D_HEADS * D_VDIM
    assert w_in.shape[1] == 3 * cw + 2 * rw + 2 * rvw and seq % ts == 0
    cos, sin = _rope_tables(seq)
    tile = lambda w: pl.BlockSpec((1, ts, w), lambda b, s: (b, s, 0))
    tab = pl.BlockSpec((ts, LANES), lambda b, s: (s, 0))
    g2, w = g[None, :], w_in.astype(BF16)
    widths = (cw, cw, cw, rw, rw, rvw, rvw)
    dtypes = (F32, F32, F32, BF16, BF16, BF16, BF16)
    return pl.pallas_call(
        functools.partial(_odd_proj_kernel, cw=cw, rw=rw, rvw=rvw),
        grid=(bsz, seq // ts),
        in_specs=[tile(d), _resident(g2), _resident(w), tab, tab],
        out_specs=[tile(wd) for wd in widths],
        out_shape=[jax.ShapeDtypeStruct((bsz, seq, wd), dt) for wd, dt in zip(widths, dtypes)],
        compiler_params=pltpu.CompilerParams(dimension_semantics=("arbitrary", "arbitrary"),
                                             vmem_limit_bytes=VMEM_LIMIT_BYTES),
        name="odd_proj",
    )(x, g2, w, cos, sin)


def _attn_kernel(q_ref, k_ref, v_ref, o_ref, num_ref, m_ref, l_ref, *, seq):
    blk = ATTN_BLOCK
    lane = lax.broadcasted_iota(jnp.int32, (blk, LANES), 1)
    head0 = lane < HEAD_DIM
    qi = lax.broadcasted_iota(jnp.int32, (2 * blk, 2 * blk), 0) & (blk - 1)
    ki = lax.broadcasted_iota(jnp.int32, (2 * blk, 2 * blk), 1)
    valid_both = ((ki < blk) & (ki >= qi)) | ((ki >= blk) & (ki - blk <= qi))
    valid_cur = (lax.broadcasted_iota(jnp.int32, (2 * blk, blk), 1)
                 <= (lax.broadcasted_iota(jnp.int32, (2 * blk, blk), 0) & (blk - 1)))

    def rows(ref, start, n, d):
        if d == 1:
            return ref[0, pl.ds(start, n), :]
        return ref[0, pl.ds(start, n, stride=d), :]

    def put(ref, idx, start, d, val):
        if d == 1:
            ref[idx, pl.ds(start, blk), :] = val
        else:
            ref[idx, pl.ds(start, blk, stride=d), :] = val

    def block(branch, q_start, d, with_prev, kv=None):
        k_start = q_start - d * blk if with_prev else q_start
        nk = 2 * blk if with_prev else blk
        q = rows(q_ref, q_start, blk, d) * LOG2_E
        if kv is None:
            k = rows(k_ref, k_start, nk, d).astype(BF16)
            v = rows(v_ref, k_start, nk, d).astype(BF16)
        else:
            k, v = kv
        q2 = jnp.concatenate([jnp.where(head0, q, 0.0), jnp.where(head0, 0.0, q)], axis=0).astype(BF16)
        s = jnp.where(valid_both if with_prev else valid_cur, _dot_nt(q2, k), NEG_BIG)
        m = jnp.max(s, axis=-1, keepdims=True)
        p = jnp.exp2(s - m)
        l = jnp.sum(p, axis=-1, keepdims=True)
        pv = _dot(p.astype(BF16), v)
        put(num_ref, branch, q_start, d, jnp.where(head0, pv[:blk], pv[blk:]))
        put(m_ref, branch, q_start, d, jnp.where(head0, m[:blk], m[blk:]))
        put(l_ref, branch, q_start, d, jnp.where(head0, l[:blk], l[blk:]))

    for branch, d in enumerate(DILATIONS):
        n_blocks = seq // (d * blk)
        step = d * blk
        subs = min(d, ATTN_INTERLEAVE[branch])
        split = ATTN_INTERLEAVE[branch] // subs
        span = n_blocks // split

        def issue(r0, n, first, branch=branch, d=d, step=step, subs=subs, split=split, span=span):
            for r in range(subs):
                for j in range(split):
                    start = r0 + r + (n + j * span) * step
                    if d == 1 and not isinstance(start, int):
                        start = pl.multiple_of(start, blk)
                    block(branch, start, d, not (first and j == 0))

        def sub_group(g, carry, issue=issue, subs=subs, span=span, branch=branch, d=d, step=step,
                      n_blocks=n_blocks):
            r0 = g * subs
            if n_blocks == 2:
                for r in range(subs):
                    k = rows(k_ref, r0 + r, 2 * blk, d).astype(BF16)
                    v = rows(v_ref, r0 + r, 2 * blk, d).astype(BF16)
                    block(branch, r0 + r, d, False, (k[:blk], v[:blk]))
                    block(branch, r0 + r + step, d, True, (k, v))
                return carry
            issue(r0, 0, True)

            def body(n, c):
                issue(r0, n, False)
                return c

            lax.fori_loop(1, span, body, 0)
            return carry

        if d == subs:
            sub_group(0, 0)
        else:
            lax.fori_loop(0, d // subs, sub_group, 0)

    def finish(i, carry):
        sl = pl.ds(pl.multiple_of(i * blk, blk), blk)
        ms = [m_ref[b, sl, :] for b in range(len(DILATIONS))]
        m_all = functools.reduce(jnp.maximum, ms)
        num = jnp.zeros((blk, LANES), F32)
        den = jnp.zeros((blk, LANES), F32)
        for b, m in enumerate(ms):
            scale = jnp.exp2(m - m_all)
            num = num + num_ref[b, sl, :] * scale
            den = den + l_ref[b, sl, :] * scale
        o_ref[0, sl, :] = (num / den).astype(o_ref.dtype)
        return carry

    lax.fori_loop(0, seq // blk, finish, 0)


def _dilated_attention(cq, ck, cv):
    bsz, seq, cw = cq.shape
    assert cw % LANES == 0 and LANES == 2 * HEAD_DIM
    for d in DILATIONS:
        assert seq % (d * ATTN_BLOCK) == 0
    spec = pl.BlockSpec((1, seq, LANES), lambda b, h: (b, 0, h))
    return pl.pallas_call(
        functools.partial(_attn_kernel, seq=seq),
        grid=(bsz, cw // LANES),
        in_specs=[spec, spec, spec],
        out_specs=spec,
        out_shape=jax.ShapeDtypeStruct((bsz, seq, cw), BF16),
        scratch_shapes=[pltpu.VMEM((len(DILATIONS), seq, LANES), F32)] * 3,
        compiler_params=pltpu.CompilerParams(dimension_semantics=("arbitrary", "arbitrary"),
                                             vmem_limit_bytes=VMEM_LIMIT_BYTES),
        name="dilated_attention",
    )(cq, ck, cv)


def _retention_tables():
    c = RET_CHUNK
    log_g = jnp.log1p(-(2.0 ** (-5.0 - jnp.arange(D_HEADS, dtype=F32))))
    i = jnp.arange(c, dtype=F32)
    diff = i[:, None] - i[None, :]
    dmat = jnp.where(diff[None] >= 0, jnp.exp(jnp.maximum(diff, 0.0)[None] * log_g[:, None, None]), 0.0)
    kdec = jnp.exp((c - 1 - i)[None, :] * log_g[:, None])
    qdec = jnp.exp((i + 1)[None, :] * log_g[:, None])
    cdec = jnp.exp(c * log_g)
    rep = lambda t: jnp.broadcast_to(t[:, :, None], (D_HEADS, c, LANES))
    return dmat, rep(kdec), rep(qdec), jnp.broadcast_to(cdec[:, None, None], (D_HEADS, SUBLANES, LANES))


def _retention_kernel(q_ref, k_ref, v_ref, g_ref, dmat_ref, kdec_ref, qdec_ref, cdec_ref, o_ref, *, seq):
    c = RET_CHUNK
    lane = lax.broadcasted_iota(jnp.int32, (c, LANES), 1)
    head0 = lane < HEAD_DIM

    def chunk(r0, states):
        q = q_ref[0, pl.ds(r0, c), :].astype(F32)
        k = k_ref[0, pl.ds(r0, c), :].astype(F32)
        new_states = []
        for hd in range(2):
            sel = head0 if hd == 0 else jnp.logical_not(head0)
            qh = jnp.where(sel, q, 0.0)
            kh = jnp.where(sel, k, 0.0)
            v = v_ref[0, pl.ds(r0, c), hd * D_VDIM:(hd + 1) * D_VDIM]
            scores = _dot_nt(qh.astype(BF16), kh.astype(BF16)) * dmat_ref[hd]
            y = _dot(scores.astype(BF16), v)
            y = y + _dot((qh * qdec_ref[hd]).astype(BF16), states[hd].astype(BF16))
            kv = _dot_tn((kh * kdec_ref[hd]).astype(BF16), v)
            new_states.append(states[hd] * cdec_ref[hd, 0:1, :] + kv)
            mu = jnp.mean(y, axis=-1, keepdims=True)
            yc = y - mu
            yn = yc * lax.rsqrt(jnp.mean(yc * yc, axis=-1, keepdims=True) + NORM_EPS)
            gate = g_ref[0, pl.ds(r0, c), hd * D_VDIM:(hd + 1) * D_VDIM].astype(F32)
            o_ref[0, pl.ds(r0, c), hd * D_VDIM:(hd + 1) * D_VDIM] = (_silu(gate) * yn).astype(o_ref.dtype)
        return tuple(new_states)

    def step(i, states):
        for j in range(RET_INTERLEAVE):
            states = chunk(pl.multiple_of((i * RET_INTERLEAVE + j) * c, c), states)
        return states

    zero = jnp.zeros((LANES, D_VDIM), F32)
    lax.fori_loop(0, seq // (c * RET_INTERLEAVE), step, (zero, zero))


def _retention(rq, rk, rv, rg):
    bsz, seq, rw = rq.shape
    assert rw % LANES == 0 and seq % (RET_CHUNK * RET_INTERLEAVE) == 0 and D_VDIM == LANES
    tabs = _retention_tables()
    n_pairs = rw // LANES
    qk_spec = pl.BlockSpec((1, seq, LANES), lambda b, h: (b, 0, h))
    v_spec = pl.BlockSpec((1, seq, 2 * D_VDIM), lambda b, h: (b, 0, h))
    tab_spec = lambda t: pl.BlockSpec((2,) + t.shape[1:], lambda b, h: (h, 0, 0))
    return pl.pallas_call(
        functools.partial(_retention_kernel, seq=seq),
        grid=(bsz, n_pairs),
        in_specs=[qk_spec, qk_spec, v_spec, v_spec] + [tab_spec(t) for t in tabs],
        out_specs=v_spec,
        out_shape=jax.ShapeDtypeStruct(rv.shape, BF16),
        compiler_params=pltpu.CompilerParams(dimension_semantics=("arbitrary", "arbitrary"),
                                             vmem_limit_bytes=VMEM_LIMIT_BYTES),
        name="retention",
    )(rq, rk, rv, rg, *tabs)


def kernel(x, ev_norm, ev_w_in, ev_a_conv, ev_a_conv_b, ev_a_ln_g, ev_a_ln_b, ev_b_conv, ev_w_out, od_norm, od_w_in, od_w_out, ffn_norm, ffn_w_up, ffn_conv, ffn_conv_b, ffn_w_down, final_norm):
    depth = ffn_norm.shape[0]
    c_width = C_HEADS * HEAD_DIM
    for layer in range(depth):
        j = layer // 2
        mixes = []
        if layer % 2 == 0:
            x = _even_mixer(x, ev_norm[j], ev_w_in[j], ev_a_conv[j], ev_a_conv_b[j], ev_a_ln_g[j], ev_a_ln_b[j],
                            ev_b_conv[j], ev_w_out[j])
        else:
            cq, ck, cv, rq, rk, rv, rg = _odd_proj(x, od_norm[j], od_w_in[j])
            mixes = [(_dilated_attention(cq, ck, cv), od_w_out[j][:c_width]),
                     (_retention(rq, rk, rv, rg), od_w_out[j][c_width:])]
        x = _conv_ffn(x, mixes, ffn_norm[layer], ffn_w_up[layer], ffn_conv[layer], ffn_conv_b[layer],
                      ffn_w_down[layer], final_norm if layer == depth - 1 else None)
    return x
```
